```python
import math
import jax, jax.numpy as jnp
from jax import lax
import numpy as np

D_MODEL = 2048
BATCH = 32
SEQ = 256
DEPTH = 4
DEC_BATCH = 8
DEC_SEQ = 2048
PAST_LEN = 512

GRID_W = 64
N_ATTN_LAYERS = (DEPTH + 1) // 2
N_SSM_LAYERS = DEPTH // 2
Q_BLOCK = 128

NA_HEADS = 8
NA_HEAD_DIM = 128
NA_WR = 8
NA_WC = 16
NA_QCB = 16
NA_KSPAN = NA_QCB + NA_WC

MLA_HEADS = 8
MLA_Q_LORA = 768
MLA_KV_LORA = 512
MLA_NOPE = 128
MLA_ROPE_DIM = 64
MLA_V = 128
ROPE_THETA = 10000.0

ATTN_IN = 3 * NA_HEADS * NA_HEAD_DIM + MLA_Q_LORA + MLA_KV_LORA + MLA_ROPE_DIM
ATTN_OUT_IN = NA_HEADS * NA_HEAD_DIM + MLA_HEADS * MLA_V
_ATTN_SPLITS = [NA_HEADS * NA_HEAD_DIM, 2 * NA_HEADS * NA_HEAD_DIM, 3 * NA_HEADS * NA_HEAD_DIM,
                3 * NA_HEADS * NA_HEAD_DIM + MLA_Q_LORA,
                3 * NA_HEADS * NA_HEAD_DIM + MLA_Q_LORA + MLA_KV_LORA]

SSM_D_INNER = 2 * D_MODEL
SSM_HEAD_DIM = 64
SSM_HEADS = SSM_D_INNER // SSM_HEAD_DIM
SSM_GROUPS = 8
SSM_STATE = 128
SSM_CONV_W = 5
SSM_CHUNK = 128
SSM_CONV_CH = SSM_D_INNER + 2 * SSM_GROUPS * SSM_STATE
SSM_IN = SSM_D_INNER + SSM_CONV_CH + 2 * SSM_HEADS
_SSM_SPLITS = [SSM_D_INNER, SSM_D_INNER + SSM_CONV_CH]

N_EXPERTS = 32
TOP_K = 4
D_FF = D_MODEL // 2
SWIGLU_LIMIT = 7.0
SWIGLU_ALPHA = 1.702
MOE_BLOCK = 128

DEEPNORM_ALPHA = (2.0 * DEPTH) ** 0.25
DEEPNORM_BETA = (8.0 * DEPTH) ** -0.25

kernel_name = 'hybrid_natten_mla_ssd_moe_diffusion_step'


def _layer_norm(x, g, b, eps=1e-5):
    xf = x.astype(jnp.float32)
    mu = jnp.mean(xf, -1, keepdims=True)
    var = jnp.mean(jnp.square(xf - mu), -1, keepdims=True)
    return ((xf - mu) * lax.rsqrt(var + eps)).astype(x.dtype) * g + b


def _rms_norm(x, g, eps=1e-6):
    xf = x.astype(jnp.float32)
    return (xf * lax.rsqrt(jnp.mean(jnp.square(xf), -1, keepdims=True) + eps)).astype(x.dtype) * g


def _rope_2d(x):
    L = x.shape[1]
    half = x.shape[-1] // 2
    quarter = half // 2
    t = jnp.arange(L)
    inv = ROPE_THETA ** (-jnp.arange(quarter, dtype=jnp.float32) / quarter)
    ang_r = (t // GRID_W).astype(jnp.float32)[:, None] * inv
    ang_c = (t % GRID_W).astype(jnp.float32)[:, None] * inv
    bshape = (L,) + (1,) * (x.ndim - 3) + (quarter,)

    def rot(v, ang):
        cos = jnp.cos(ang).reshape(bshape).astype(v.dtype)
        sin = jnp.sin(ang).reshape(bshape).astype(v.dtype)
        v1, v2 = v[..., :quarter], v[..., quarter:]
        return jnp.concatenate([v1 * cos - v2 * sin, v2 * cos + v1 * sin], -1)

    return jnp.concatenate([rot(x[..., :half], ang_r), rot(x[..., half:], ang_c)], -1)


def _block_sweep(fn, *q_args):
    b, lq = q_args[0].shape[:2]
    nb = lq // Q_BLOCK
    blocks = tuple(jnp.moveaxis(a.reshape((b, nb, Q_BLOCK) + a.shape[2:]), 1, 0) for a in q_args)
    out = lax.map(lambda args: fn(*args), blocks)
    return jnp.moveaxis(out, 0, 1).reshape((b, lq) + out.shape[3:])


def _dense_attend(q, k, v):
    scale = q.shape[-1] ** -0.5

    def blk(qb):
        s = jnp.einsum('bqhd,bkhd->bhqk', qb, k).astype(jnp.float32) * scale
        p = jax.nn.softmax(s, -1).astype(v.dtype)
        return jnp.einsum('bhqk,bkhd->bqhd', p, v)

    return _block_sweep(blk, q)


def _mla_attend(q_nope, q_pe, k_nope, k_pe, v):
    scale = (MLA_NOPE + MLA_ROPE_DIM) ** -0.5

    def blk(qn, qp):
        s = (jnp.einsum('bqhd,bkhd->bhqk', qn, k_nope)
             + jnp.einsum('bqhr,bkr->bhqk', qp, k_pe)).astype(jnp.float32) * scale
        p = jax.nn.softmax(s, -1).astype(v.dtype)
        return jnp.einsum('bhqk,bkhd->bqhd', p, v)

    return _block_sweep(blk, q_nope, q_pe)


def _na_latent(q, k, v, k_ctx, v_ctx, rpb):
    b, L, h, d = q.shape
    rows = L // GRID_W
    wr = min(NA_WR, rows)
    ncb = GRID_W // NA_QCB
    scale = d ** -0.5
    kcs = np.clip(np.arange(ncb) * NA_QCB - NA_WC // 2, 0, GRID_W - NA_KSPAN)
    kcol = kcs[:, None] + np.arange(NA_KSPAN)
    qcol = np.arange(GRID_W).reshape(ncb, NA_QCB)
    qcs = np.clip(qcol - NA_WC // 2, 0, GRID_W - NA_WC)
    col_ok = (kcol[:, None, :] >= qcs[:, :, None]) & (kcol[:, None, :] < qcs[:, :, None] + NA_WC)
    col_idx = np.clip(kcol[:, None, :] - qcol[:, :, None] + NA_WC - 1, 0, 2 * NA_WC - 2)
    col_bias = rpb[:, :, col_idx]
    mask = jnp.asarray(col_ok)[:, None, :, None, :]
    kg = k.reshape(b, rows, GRID_W, h, d)
    vg = v.reshape(b, rows, GRID_W, h, d)
    qg = jnp.moveaxis(q.reshape(b, rows, ncb, NA_QCB, h, d), 1, 0)
    n_loc = wr * NA_KSPAN

    def one_row(args):
        r, qr = args
        rs = jnp.clip(r - NA_WR // 2, 0, rows - wr)
        kw = lax.dynamic_slice_in_dim(kg, rs, wr, axis=1)[:, :, kcol]
        vw = lax.dynamic_slice_in_dim(vg, rs, wr, axis=1)[:, :, kcol]
        row_idx = rs + jnp.arange(wr) - r + NA_WR - 1
        bias = jnp.transpose(jnp.take(col_bias, row_idx, axis=1), (2, 0, 3, 1, 4))
        s_loc = jnp.einsum('bjihd,bwjmhd->bjhiwm', qr, kw).astype(jnp.float32) * scale + bias.astype(jnp.float32)
        s_loc = jnp.where(mask, s_loc, -1e30).reshape(b, ncb, h, NA_QCB, n_loc)
        s_ctx = jnp.einsum('bjihd,bkhd->bjhik', qr, k_ctx).astype(jnp.float32) * scale
        p = jax.nn.softmax(jnp.concatenate([s_loc, s_ctx], -1), -1).astype(v.dtype)
        p_loc = p[..., :n_loc].reshape(b, ncb, h, NA_QCB, wr, NA_KSPAN)
        p_ctx = p[..., n_loc:]
        return (jnp.einsum('bjhiwm,bwjmhd->bjihd', p_loc, vw)
                + jnp.einsum('bjhik,bkhd->bjihd', p_ctx, v_ctx))

    out = lax.map(one_row, (jnp.arange(rows), qg))
    return jnp.moveaxis(out, 0, 1).reshape(b, L, h, d)


def _attn_projections(h, w_in, q_norm, w_uq, kv_norm):
    b, L, _ = h.shape
    na_q, na_k, na_v, cq, ckv, kpe = jnp.split(h @ w_in, _ATTN_SPLITS, axis=-1)
    heads = lambda t: t.reshape(b, L, NA_HEADS, NA_HEAD_DIM)
    q = (_rms_norm(cq, q_norm) @ w_uq).reshape(b, L, MLA_HEADS, MLA_NOPE + MLA_ROPE_DIM)
    return (heads(na_q), heads(na_k), heads(na_v), q[..., :MLA_NOPE], q[..., MLA_NOPE:],
            _rms_norm(ckv, kv_norm), kpe)


def _mla_expand(ckv, w_ukv):
    b, L, _ = ckv.shape
    kv = (ckv @ w_ukv).reshape(b, L, MLA_HEADS, MLA_NOPE + MLA_V)
    return kv[..., :MLA_NOPE], kv[..., MLA_NOPE:]


def _merge_out(o_na, o_mla, w_out):
    b, L = o_na.shape[:2]
    return jnp.concatenate([o_na.reshape(b, L, -1), o_mla.reshape(b, L, -1)], -1) @ w_out


def _attn_context(h, w_in, q_norm, w_uq, kv_norm, w_ukv, w_out):
    na_q, na_k, na_v, q_nope, q_pe, ckv, kpe = _attn_projections(h, w_in, q_norm, w_uq, kv_norm)
    o_na = _dense_attend(na_q, na_k, na_v)
    k_nope, v = _mla_expand(ckv, w_ukv)
    o_mla = _mla_attend(q_nope, q_pe, k_nope, kpe, v)
    return _merge_out(o_na, o_mla, w_out), (na_k, na_v, ckv, kpe)


def _attn_latent(h, c_k, c_v, c_ckv, c_kpe, w_in, rpb, q_norm, w_uq, kv_norm, w_ukv, w_out):
    na_q, na_k, na_v, q_nope, q_pe, ckv, kpe = _attn_projections(h, w_in, q_norm, w_uq, kv_norm)
    o_na = _na_latent(na_q, na_k, na_v, c_k, c_v, rpb)
    k_nope, v = _mla_expand(ckv, w_ukv)
    ck_nope, cv = _mla_expand(c_ckv, w_ukv)
    o_mla = _mla_attend(q_nope, _rope_2d(q_pe),
                        jnp.concatenate([k_nope, ck_nope], 1),
                        jnp.concatenate([_rope_2d(kpe), c_kpe], 1),
                        jnp.concatenate([v, cv], 1))
    return _merge_out(o_na, o_mla, w_out)


def _conv_centred(x, w, b):
    pad = (SSM_CONV_W - 1) // 2
    y = lax.conv_general_dilated(x, w[:, None, :], (1,), [(pad, pad)],
                                 dimension_numbers=('NWC', 'WIO', 'NWC'),
                                 feature_group_count=x.shape[-1])
    return y + b


def _ssd_scan(x, dt, a, bm, cm, h0):
    b, L, nh, p = x.shape
    g, n = bm.shape[2:]
    r = nh // g
    nc = L // SSM_CHUNK
    xc = x.reshape(b, nc, SSM_CHUNK, g, r, p)
    dtc = dt.reshape(b, nc, SSM_CHUNK, g, r)
    bc = bm.reshape(b, nc, SSM_CHUNK, g, n)
    cc = cm.reshape(b, nc, SSM_CHUNK, g, n)
    acs = jnp.cumsum(dtc * a.reshape(g, r), axis=2)
    tri = jnp.tril(jnp.ones((SSM_CHUNK, SSM_CHUNK), bool))[:, :, None, None]
    seg = acs[:, :, :, None] - acs[:, :, None]
    decay = jnp.where(tri, jnp.exp(jnp.where(tri, seg, 0.0)), 0.0)
    cb = jnp.einsum('bcign,bcjgn->bcijg', cc, bc)
    y_diag = jnp.einsum('bcijg,bcijgr,bcjgrp->bcigrp', cb, decay * dtc[:, :, None], xc)
    w_end = jnp.exp(acs[:, :, -1:] - acs) * dtc
    states = jnp.einsum('bcjgn,bcjgr,bcjgrp->bcgrpn', bc, w_end, xc)
    chunk_decay = jnp.exp(acs[:, :, -1])

    def step(hc, inp):
        st, dc = inp
        return hc * dc[..., None, None] + st, hc

    h_last, h_before = lax.scan(step, h0.astype(jnp.float32).reshape(b, g, r, p, n),
                                (jnp.moveaxis(states, 1, 0), jnp.moveaxis(chunk_decay, 1, 0)))
    h_before = jnp.moveaxis(h_before, 0, 1)
    y_off = jnp.einsum('bcign,bcgrpn,bcigr->bcigrp', cc, h_before, jnp.exp(acs))
    y = (y_diag + y_off).reshape(b, L, nh, p).astype(x.dtype)
    return y, h_last.reshape(b, nh, p, n)


def _ssm_mixer(h, h0, w_in, conv_w, conv_b, dt_bias, a_log, d_skip, norm_w, w_out):
    b, L, _ = h.shape
    z, xbc, dt_raw = jnp.split(h @ w_in, _SSM_SPLITS, axis=-1)
    xbc = jax.nn.silu(_conv_centred(xbc, conv_w, conv_b))
    xs, bm, cm = jnp.split(xbc, [SSM_D_INNER, SSM_D_INNER + SSM_GROUPS * SSM_STATE], axis=-1)
    xs = xs.reshape(b, L, SSM_HEADS, SSM_HEAD_DIM)
    bm = bm.reshape(b, L, SSM_GROUPS, SSM_STATE)
    cm = cm.reshape(b, L, SSM_GROUPS, SSM_STATE)
    dt = jax.nn.softplus(dt_raw.reshape(b, L, 2, SSM_HEADS).astype(jnp.float32) + dt_bias.astype(jnp.float32))
    a = -jnp.exp(a_log.astype(jnp.float32))
    flip = lambda t: jnp.flip(t, axis=1)
    y_f, h_f = _ssd_scan(xs, dt[:, :, 0], a[0], bm, cm, h0[:, 0])
    y_b, h_b = _ssd_scan(flip(xs), flip(dt[:, :, 1]), a[1], flip(bm), flip(cm), h0[:, 1])
    y = y_f + d_skip[0][:, None] * xs + flip(y_b) + d_skip[1][:, None] * xs
    y = y.reshape(b, L, SSM_D_INNER) * jax.nn.silu(z)
    y = _rms_norm(y.reshape(b, L, SSM_GROUPS, -1), norm_w.reshape(SSM_GROUPS, -1), eps=1e-5)
    return y.reshape(b, L, SSM_D_INNER) @ w_out, jnp.stack([h_f, h_b], axis=1).astype(h.dtype)


def _moe(h, w_router, b_router, w_gu, b_gu, w_down, b_down):
    b, L, d = h.shape
    x = h.reshape(-1, d)
    T = x.shape[0]
    logits = (x @ w_router + b_router).astype(jnp.float32)
    top_v, top_e = lax.top_k(logits, TOP_K)
    gates = jax.nn.softmax(top_v, -1).astype(x.dtype)
    n = T * TOP_K
    flat_e = top_e.reshape(-1)
    order = jnp.argsort(flat_e)
    sorted_e = flat_e[order]
    counts = jnp.zeros((N_EXPERTS,), jnp.int32).at[flat_e].add(1)
    padded = (counts + MOE_BLOCK - 1) // MOE_BLOCK * MOE_BLOCK
    pad_end = jnp.cumsum(padded)
    pad_start = pad_end - padded
    start = jnp.cumsum(counts) - counts
    dest = pad_start[sorted_e] + jnp.arange(n, dtype=jnp.int32) - start[sorted_e]
    nb = -(-n // MOE_BLOCK) + N_EXPERTS
    tok = order // TOP_K
    row_tok = jnp.zeros((nb * MOE_BLOCK,), jnp.int32).at[dest].set(tok)
    block_e = jnp.minimum(jnp.searchsorted(pad_end, jnp.arange(nb, dtype=jnp.int32) * MOE_BLOCK, side='right'),
                          N_EXPERTS - 1)
    xb = x[row_tok].reshape(nb, MOE_BLOCK, d)

    def expert_block(args):
        xblk, e = args
        gu = xblk @ w_gu[e] + b_gu[e]
        x_glu = jnp.minimum(gu[:, :D_FF], SWIGLU_LIMIT)
        x_lin = jnp.clip(gu[:, D_FF:], -SWIGLU_LIMIT, SWIGLU_LIMIT)
        act = x_glu * jax.nn.sigmoid(SWIGLU_ALPHA * x_glu) * (x_lin + 1.0)
        return act @ w_down[e] + b_down[e]

    yb = lax.map(expert_block, (xb, block_e)).reshape(nb * MOE_BLOCK, d)
    y_assign = yb[dest] * gates.reshape(-1)[order][:, None]
    out = jnp.zeros_like(x).at[tok].add(y_assign)
    return out.reshape(b, L, d)


def _modulation(cvec, w, b):
    m = jax.nn.silu(cvec) @ w + b
    return [t[:, None, :] for t in jnp.split(m, 6, axis=-1)]


def _modulate(x, shift, scale):
    return x * (1.0 + scale) + shift


def _residual(x, sub, gate, g, b):
    return _layer_norm(DEEPNORM_ALPHA * x + gate * sub, g, b)


def setup_inputs(seed: int = 0) -> dict:
    key = jax.random.key(seed)
    ks = iter(jax.random.split(key, 48))
    f32 = jnp.float32
    nrm = lambda shape, s: jax.random.normal(next(ks), shape, f32) * s
    D = D_MODEL
    dt_init = jnp.exp(jax.random.uniform(next(ks), (N_SSM_LAYERS, 2, SSM_HEADS), f32,
                                         math.log(1e-3), math.log(1e-1)))
    return {
        'x_prompt': nrm((BATCH, SEQ, D), 1.0),
        'x_sample': nrm((DEC_BATCH, DEC_SEQ, D), 1.0),
        'c': nrm((DEC_BATCH, D), 1.0),
        'cache_na_k': nrm((DEC_BATCH, N_ATTN_LAYERS, PAST_LEN, NA_HEADS, NA_HEAD_DIM), 1.0),
        'cache_na_v': nrm((DEC_BATCH, N_ATTN_LAYERS, PAST_LEN, NA_HEADS, NA_HEAD_DIM), 1.0),
        'cache_mla_ckv': nrm((DEC_BATCH, N_ATTN_LAYERS, PAST_LEN, MLA_KV_LORA), 1.0),
        'cache_mla_kpe': nrm((DEC_BATCH, N_ATTN_LAYERS, PAST_LEN, MLA_ROPE_DIM), 1.0),
        'state_ssm': nrm((DEC_BATCH, N_SSM_LAYERS, 2, SSM_HEADS, SSM_HEAD_DIM, SSM_STATE), 0.1),
        'c_ctx': nrm((D,), 1.0),
        'w_ada': nrm((DEPTH, D, 6 * D), 0.5 * D ** -0.5),
        'b_ada': nrm((DEPTH, 6 * D), 0.02),
        'ln_g': 1.0 + nrm((DEPTH, 2, D), 0.02),
        'ln_b': nrm((DEPTH, 2, D), 0.02),
        'attn_w_in': nrm((N_ATTN_LAYERS, D, ATTN_IN), D ** -0.5),
        'na_rpb': nrm((N_ATTN_LAYERS, NA_HEADS, 2 * NA_WR - 1, 2 * NA_WC - 1), 0.1),
        'mla_q_norm': 1.0 + nrm((N_ATTN_LAYERS, MLA_Q_LORA), 0.02),
        'mla_w_uq': nrm((N_ATTN_LAYERS, MLA_Q_LORA, MLA_HEADS * (MLA_NOPE + MLA_ROPE_DIM)), MLA_Q_LORA ** -0.5),
        'mla_kv_norm': 1.0 + nrm((N_ATTN_LAYERS, MLA_KV_LORA), 0.02),
        'mla_w_ukv': nrm((N_ATTN_LAYERS, MLA_KV_LORA, MLA_HEADS * (MLA_NOPE + MLA_V)), MLA_KV_LORA ** -0.5),
        'attn_w_out': nrm((N_ATTN_LAYERS, ATTN_OUT_IN, D), DEEPNORM_BETA * ATTN_OUT_IN ** -0.5),
        'ssm_w_in': nrm((N_SSM_LAYERS, D, SSM_IN), D ** -0.5),
        'ssm_conv_w': nrm((N_SSM_LAYERS, SSM_CONV_W, SSM_CONV_CH), SSM_CONV_W ** -0.5),
        'ssm_conv_b': nrm((N_SSM_LAYERS, SSM_CONV_CH), 0.02),
        'ssm_dt_bias': dt_init + jnp.log(-jnp.expm1(-dt_init)),
        'ssm_a_log': jnp.log(jax.random.uniform(next(ks), (N_SSM_LAYERS, 2, SSM_HEADS), f32, 1.0, 16.0)),
        'ssm_d': 1.0 + nrm((N_SSM_LAYERS, 2, SSM_HEADS), 0.1),
        'ssm_norm': 1.0 + nrm((N_SSM_LAYERS, SSM_D_INNER), 0.02),
        'ssm_w_out': nrm((N_SSM_LAYERS, SSM_D_INNER, D), DEEPNORM_BETA * SSM_D_INNER ** -0.5),
        'moe_w_router': nrm((DEPTH, D, N_EXPERTS), D ** -0.5),
        'moe_b_router': nrm((DEPTH, N_EXPERTS), 0.01),
        'moe_w_gu': nrm((DEPTH, N_EXPERTS, D, 2 * D_FF), D ** -0.5),
        'moe_b_gu': nrm((DEPTH, N_EXPERTS, 2 * D_FF), 0.02),
        'moe_w_down': nrm((DEPTH, N_EXPERTS, D_FF, D), DEEPNORM_BETA * D_FF ** -0.5),
        'moe_b_down': nrm((DEPTH, N_EXPERTS, D), 0.02),
    }


def reference(x_prompt, x_sample, c, cache_na_k, cache_na_v, cache_mla_ckv, cache_mla_kpe, state_ssm,
              c_ctx, w_ada, b_ada, ln_g, ln_b, attn_w_in, na_rpb, mla_q_norm, mla_w_uq, mla_kv_norm,
              mla_w_ukv, attn_w_out, ssm_w_in, ssm_conv_w, ssm_conv_b, ssm_dt_bias, ssm_a_log, ssm_d,
              ssm_norm, ssm_w_out, moe_w_router, moe_b_router, moe_w_gu, moe_b_gu, moe_w_down, moe_b_down):
    xp, xs = x_prompt, x_sample
    b_p = x_prompt.shape[0]
    new_k, new_v, new_ckv, new_kpe, new_ssm = [], [], [], [], []
    for l in range(DEPTH):
        mc = _modulation(c_ctx[None, :], w_ada[l], b_ada[l])
        ml = _modulation(c, w_ada[l], b_ada[l])
        hp = _modulate(xp, mc[0], mc[1])
        hs = _modulate(xs, ml[0], ml[1])
        if l % 2 == 0:
            i = l // 2
            yp, (k_c, v_c, ckv_c, kpe_c) = _attn_context(hp, attn_w_in[i], mla_q_norm[i], mla_w_uq[i],
                                                         mla_kv_norm[i], mla_w_ukv[i], attn_w_out[i])
            ys = _attn_latent(hs, cache_na_k[:, i], cache_na_v[:, i], cache_mla_ckv[:, i], cache_mla_kpe[:, i],
                              attn_w_in[i], na_rpb[i], mla_q_norm[i], mla_w_uq[i], mla_kv_norm[i],
                              mla_w_ukv[i], attn_w_out[i])
            new_k.append(k_c)
            new_v.append(v_c)
            new_ckv.append(ckv_c)
            new_kpe.append(kpe_c)
        else:
            j = l // 2
            zero_state = jnp.zeros((b_p, 2, SSM_HEADS, SSM_HEAD_DIM, SSM_STATE), xp.dtype)
            ssm_params = (ssm_w_in[j], ssm_conv_w[j], ssm_conv_b[j], ssm_dt_bias[j], ssm_a_log[j],
                          ssm_d[j], ssm_norm[j], ssm_w_out[j])
            yp, st = _ssm_mixer(hp, zero_state, *ssm_params)
            ys, _ = _ssm_mixer(hs, state_ssm[:, j], *ssm_params)
            new_ssm.append(st)
        xp = _residual(xp, yp, mc[2], ln_g[l, 0], ln_b[l, 0])
        xs = _residual(xs, ys, ml[2], ln_g[l, 0], ln_b[l, 0])
        moe_params = (moe_w_router[l], moe_b_router[l], moe_w_gu[l], moe_b_gu[l], moe_w_down[l], moe_b_down[l])
        xp = _residual(xp, _moe(_modulate(xp, mc[3], mc[4]), *moe_params), mc[5], ln_g[l, 1], ln_b[l, 1])
        xs = _residual(xs, _moe(_modulate(xs, ml[3], ml[4]), *moe_params), ml[5], ln_g[l, 1], ln_b[l, 1])
    return (xp, xs, jnp.stack(new_k, 1), jnp.stack(new_v, 1), jnp.stack(new_ckv, 1),
            jnp.stack(new_kpe, 1), jnp.stack(new_ssm, 1))
```

```python
import functools
import math

import jax
import jax.numpy as jnp
import numpy as np
from jax import lax
from jax.experimental import pallas as pl
from jax.experimental.pallas import tpu as pltpu

D_MODEL = 2048
DEPTH = 4
GRID_W = 64
Q_BLOCK = 128

NA_HEADS = 8
NA_HEAD_DIM = 128
NA_WR = 8
NA_WC = 16
NA_QCB = 16
NA_KSPAN = NA_QCB + NA_WC

MLA_HEADS = 8
MLA_Q_LORA = 768
MLA_KV_LORA = 512
MLA_NOPE = 128
MLA_ROPE_DIM = 64
MLA_V = 128
ROPE_THETA = 10000.0

_ATTN_SPLITS = [NA_HEADS * NA_HEAD_DIM, 2 * NA_HEADS * NA_HEAD_DIM, 3 * NA_HEADS * NA_HEAD_DIM,
                3 * NA_HEADS * NA_HEAD_DIM + MLA_Q_LORA,
                3 * NA_HEADS * NA_HEAD_DIM + MLA_Q_LORA + MLA_KV_LORA]

SSM_D_INNER = 2 * D_MODEL
SSM_HEAD_DIM = 64
SSM_HEADS = SSM_D_INNER // SSM_HEAD_DIM
SSM_GROUPS = 8
SSM_STATE = 128
SSM_CONV_W = 5
SSM_CHUNK = 128
SSM_CONV_CH = SSM_D_INNER + 2 * SSM_GROUPS * SSM_STATE
_SSM_SPLITS = [SSM_D_INNER, SSM_D_INNER + SSM_CONV_CH]

N_EXPERTS = 32
TOP_K = 4
D_FF = D_MODEL // 2
SWIGLU_LIMIT = 7.0
SWIGLU_ALPHA = 1.702

DEEPNORM_ALPHA = (2.0 * DEPTH) ** 0.25

V7X_VMEM_LIMIT_BYTES = 48 * 1024 * 1024
MM_TILE_M = 1024
MM_TILE_N = 1024
MOE_TILE_M = 512

BF16 = jnp.bfloat16
F32 = jnp.float32


def _mm_kernel(x_ref, w_ref, o_ref):
    o_ref[...] = jnp.dot(x_ref[...], w_ref[...], preferred_element_type=F32).astype(o_ref.dtype)


def _pick_tile(n, pref):
    for t in (pref, 768, 512, 384, 256, 128):
        if t <= pref and n % t == 0:
            return t
    return n


def _matmul(x, w, out_dtype=F32):
    m, k = x.shape
    n = w.shape[1]
    tm = _pick_tile(m, MM_TILE_M)
    tn = _pick_tile(n, MM_TILE_N)
    return pl.pallas_call(
        _mm_kernel,
        grid=(m // tm, n // tn),
        in_specs=[pl.BlockSpec((tm, k), lambda i, j: (i, 0)),
                  pl.BlockSpec((k, tn), lambda i, j: (0, j))],
        out_specs=pl.BlockSpec((tm, tn), lambda i, j: (i, j)),
        out_shape=jax.ShapeDtypeStruct((m, n), out_dtype),
        compiler_params=pltpu.CompilerParams(
            dimension_semantics=("parallel", "parallel"),
            vmem_limit_bytes=V7X_VMEM_LIMIT_BYTES),
    )(x.astype(BF16), w.astype(BF16))


def _pad_cols(w, mult):
    n = w.shape[-1]
    pad = (-n) % mult
    if pad:
        w = jnp.pad(w, [(0, 0)] * (w.ndim - 1) + [(0, pad)])
    return w


def _router_kernel(x_ref, w_ref, b_ref, o_ref):
    o_ref[...] = jnp.dot(x_ref[...], w_ref[...], preferred_element_type=F32,
                         precision=lax.Precision.HIGHEST) + b_ref[...]


def _router_logits(x, w, b):
    m, k = x.shape
    n = w.shape[1]
    tm = 512
    return pl.pallas_call(
        _router_kernel,
        grid=(m // tm,),
        in_specs=[pl.BlockSpec((tm, k), lambda i: (i, 0)),
                  pl.BlockSpec((k, n), lambda i: (0, 0)),
                  pl.BlockSpec((1, n), lambda i: (0, 0))],
        out_specs=pl.BlockSpec((tm, n), lambda i: (i, 0)),
        out_shape=jax.ShapeDtypeStruct((m, n), F32),
        compiler_params=pltpu.CompilerParams(
            dimension_semantics=("parallel",),
            vmem_limit_bytes=V7X_VMEM_LIMIT_BYTES),
    )(x, w, b.reshape(1, n))


def _moe_ffn_kernel(be_ref, nused_ref, x_ref, wgu_ref, bgu_ref, wd_ref, bd_ref, o_ref):
    @pl.when(pl.program_id(0) < nused_ref[0])
    def _():
        x = x_ref[...]
        gu = jnp.dot(x, wgu_ref[0], preferred_element_type=F32) + bgu_ref[0]
        x_glu = jnp.minimum(gu[:, :D_FF], SWIGLU_LIMIT)
        x_lin = jnp.clip(gu[:, D_FF:], -SWIGLU_LIMIT, SWIGLU_LIMIT)
        act = x_glu * jax.nn.sigmoid(SWIGLU_ALPHA * x_glu) * (x_lin + 1.0)
        o_ref[...] = jnp.dot(act.astype(BF16), wd_ref[0], preferred_element_type=F32) + bd_ref[0]


def _moe_ffn(xb, block_e, n_used, w_gu, b_gu, w_down, b_down):
    rows, d = xb.shape
    nb = rows // MOE_TILE_M
    grid_spec = pltpu.PrefetchScalarGridSpec(
        num_scalar_prefetch=2,
        grid=(nb,),
        in_specs=[pl.BlockSpec((MOE_TILE_M, d), lambda i, be, nu: (i, 0)),
                  pl.BlockSpec((1, d, 2 * D_FF), lambda i, be, nu: (be[i], 0, 0)),
                  pl.BlockSpec((1, 1, 2 * D_FF), lambda i, be, nu: (be[i], 0, 0)),
                  pl.BlockSpec((1, D_FF, d), lambda i, be, nu: (be[i], 0, 0)),
                  pl.BlockSpec((1, 1, d), lambda i, be, nu: (be[i], 0, 0))],
        out_specs=pl.BlockSpec((MOE_TILE_M, d), lambda i, be, nu: (i, 0)),
    )
    return pl.pallas_call(
        _moe_ffn_kernel,
        grid_spec=grid_spec,
        out_shape=jax.ShapeDtypeStruct((rows, d), F32),
        compiler_params=pltpu.CompilerParams(
            dimension_semantics=("arbitrary",),
            vmem_limit_bytes=V7X_VMEM_LIMIT_BYTES),
    )(block_e, n_used, xb, w_gu, b_gu.reshape(N_EXPERTS, 1, 2 * D_FF), w_down,
      b_down.reshape(N_EXPERTS, 1, d))


def _moe(x, w_router, b_router, w_gu, b_gu, w_down, b_down):
    T, d = x.shape
    tm = MOE_TILE_M
    logits = _router_logits(x, w_router, b_router)
    top_v, top_e = lax.top_k(logits, TOP_K)
    gates = jax.nn.softmax(top_v, -1)
    n = T * TOP_K
    flat_e = top_e.reshape(-1)
    onehot = (flat_e[:, None] == jnp.arange(N_EXPERTS, dtype=jnp.int32)[None, :]).astype(jnp.int32)
    csum = jnp.cumsum(onehot, axis=0)
    counts = csum[-1]
    rank = jnp.sum((csum - 1) * onehot, axis=1)
    padded = (counts + tm - 1) // tm * tm
    pad_end = jnp.cumsum(padded)
    pad_start = pad_end - padded
    start = jnp.cumsum(counts) - counts
    pos = pad_start[flat_e] + rank
    nb = n // tm + N_EXPERTS
    order = jnp.argsort(flat_e)
    n_used = (pad_end[-1] // tm).astype(jnp.int32)
    blk = jnp.arange(nb, dtype=jnp.int32)
    block_e = jnp.minimum(jnp.searchsorted(pad_end, blk * tm, side='right'), N_EXPERTS - 1).astype(jnp.int32)
    last_e = block_e[jnp.maximum(n_used - 1, 0)]
    block_e = jnp.where(blk < n_used, block_e, last_e)
    r = jnp.arange(nb * tm, dtype=jnp.int32)
    row_e = jnp.repeat(block_e, tm)
    off = r - pad_start[row_e]
    valid = (off < counts[row_e]) & (r < pad_end[-1])
    src = jnp.clip(start[row_e] + off, 0, n - 1)
    row_tok = jnp.where(valid, order[src] // TOP_K, 0)
    xb = x.astype(BF16)[row_tok]
    yb = _moe_ffn(xb, block_e, n_used.reshape(1), w_gu, b_gu, w_down, b_down)
    y = yb[pos].reshape(T, TOP_K, d) * gates[..., None]
    return jnp.sum(y, axis=1)


def _layer_norm(x, g, b, eps=1e-5):
    mu = jnp.mean(x, -1, keepdims=True)
    var = jnp.mean(jnp.square(x - mu), -1, keepdims=True)
    return ((x - mu) * lax.rsqrt(var + eps)) * g + b


def _rms_norm(x, g, eps=1e-6):
    return (x * lax.rsqrt(jnp.mean(jnp.square(x), -1, keepdims=True) + eps)) * g


def _rope_2d(x):
    L = x.shape[1]
    half = x.shape[-1] // 2
    quarter = half // 2
    t = jnp.arange(L)
    inv = ROPE_THETA ** (-jnp.arange(quarter, dtype=F32) / quarter)
    ang_r = (t // GRID_W).astype(F32)[:, None] * inv
    ang_c = (t % GRID_W).astype(F32)[:, None] * inv
    bshape = (L,) + (1,) * (x.ndim - 3) + (quarter,)

    def rot(v, ang):
        cos = jnp.cos(ang).reshape(bshape).astype(v.dtype)
        sin = jnp.sin(ang).reshape(bshape).astype(v.dtype)
        v1, v2 = v[..., :quarter], v[..., quarter:]
        return jnp.concatenate([v1 * cos - v2 * sin, v2 * cos + v1 * sin], -1)

    return jnp.concatenate([rot(x[..., :half], ang_r), rot(x[..., half:], ang_c)], -1)


def _block_sweep(fn, *q_args):
    b, lq = q_args[0].shape[:2]
    nb = lq // Q_BLOCK
    blocks = tuple(jnp.moveaxis(a.reshape((b, nb, Q_BLOCK) + a.shape[2:]), 1, 0) for a in q_args)
    out = lax.map(lambda args: fn(*args), blocks)
    return jnp.moveaxis(out, 0, 1).reshape((b, lq) + out.shape[3:])


def _dense_attend(q, k, v):
    scale = q.shape[-1] ** -0.5

    def blk(qb):
        s = jnp.einsum('bqhd,bkhd->bhqk', qb, k).astype(F32) * scale
        p = jax.nn.softmax(s, -1).astype(v.dtype)
        return jnp.einsum('bhqk,bkhd->bqhd', p, v)

    return _block_sweep(blk, q)


def _mla_attend(q_nope, q_pe, k_nope, k_pe, v):
    scale = (MLA_NOPE + MLA_ROPE_DIM) ** -0.5

    def blk(qn, qp):
        s = (jnp.einsum('bqhd,bkhd->bhqk', qn, k_nope)
             + jnp.einsum('bqhr,bkr->bhqk', qp, k_pe)).astype(F32) * scale
        p = jax.nn.softmax(s, -1).astype(v.dtype)
        return jnp.einsum('bhqk,bkhd->bqhd', p, v)

    return _block_sweep(blk, q_nope, q_pe)


def _na_latent(q, k, v, k_ctx, v_ctx, rpb):
    b, L, h, d = q.shape
    rows = L // GRID_W
    wr = min(NA_WR, rows)
    ncb = GRID_W // NA_QCB
    scale = d ** -0.5
    kcs = np.clip(np.arange(ncb) * NA_QCB - NA_WC // 2, 0, GRID_W - NA_KSPAN)
    kcol = kcs[:, None] + np.arange(NA_KSPAN)
    qcol = np.arange(GRID_W).reshape(ncb, NA_QCB)
    qcs = np.clip(qcol - NA_WC // 2, 0, GRID_W - NA_WC)
    col_ok = (kcol[:, None, :] >= qcs[:, :, None]) & (kcol[:, None, :] < qcs[:, :, None] + NA_WC)
    col_idx = np.clip(kcol[:, None, :] - qcol[:, :, None] + NA_WC - 1, 0, 2 * NA_WC - 2)
    col_bias = rpb[:, :, col_idx]
    mask = jnp.asarray(col_ok)[:, None, :, None, :]
    kg = k.reshape(b, rows, GRID_W, h, d)
    vg = v.reshape(b, rows, GRID_W, h, d)
    qg = jnp.moveaxis(q.reshape(b, rows, ncb, NA_QCB, h, d), 1, 0)
    n_loc = wr * NA_KSPAN

    def one_row(args):
        r, qr = args
        rs = jnp.clip(r - NA_WR // 2, 0, rows - wr)
        kw = lax.dynamic_slice_in_dim(kg, rs, wr, axis=1)[:, :, kcol]
        vw = lax.dynamic_slice_in_dim(vg, rs, wr, axis=1)[:, :, kcol]
        row_idx = rs + jnp.arange(wr) - r + NA_WR - 1
        bias = jnp.transpose(jnp.take(col_bias, row_idx, axis=1), (2, 0, 3, 1, 4))
        s_loc = jnp.einsum('bjihd,bwjmhd->bjhiwm', qr, kw).astype(F32) * scale + bias.astype(F32)
        s_loc = jnp.where(mask, s_loc, -1e30).reshape(b, ncb, h, NA_QCB, n_loc)
        s_ctx = jnp.einsum('bjihd,bkhd->bjhik', qr, k_ctx).astype(F32) * scale
        p = jax.nn.softmax(jnp.concatenate([s_loc, s_ctx], -1), -1).astype(v.dtype)
        p_loc = p[..., :n_loc].reshape(b, ncb, h, NA_QCB, wr, NA_KSPAN)
        p_ctx = p[..., n_loc:]
        return (jnp.einsum('bjhiwm,bwjmhd->bjihd', p_loc, vw)
                + jnp.einsum('bjhik,bkhd->bjihd', p_ctx, v_ctx))

    out = lax.map(one_row, (jnp.arange(rows), qg))
    return jnp.moveaxis(out, 0, 1).reshape(b, L, h, d)


def _conv_centred(x, w, b):
    pad = (SSM_CONV_W - 1) // 2
    y = lax.conv_general_dilated(x, w[:, None, :], (1,), [(pad, pad)],
                                 dimension_numbers=('NWC', 'WIO', 'NWC'),
                                 feature_group_count=x.shape[-1])
    return y + b


def _ssd_scan(x, dt, a, bm, cm, h0):
    b, L, nh, p = x.shape
    g, n = bm.shape[2:]
    r = nh // g
    nc = L // SSM_CHUNK
    xc = x.reshape(b, nc, SSM_CHUNK, g, r, p)
    dtc = dt.reshape(b, nc, SSM_CHUNK, g, r)
    bc = bm.reshape(b, nc, SSM_CHUNK, g, n)
    cc = cm.reshape(b, nc, SSM_CHUNK, g, n)
    acs = jnp.cumsum(dtc * a.reshape(g, r), axis=2)
    tri = jnp.tril(jnp.ones((SSM_CHUNK, SSM_CHUNK), bool))[:, :, None, None]
    seg = acs[:, :, :, None] - acs[:, :, None]
    decay = jnp.where(tri, jnp.exp(jnp.where(tri, seg, 0.0)), 0.0)
    cb = jnp.einsum('bcign,bcjgn->bcijg', cc, bc)
    y_diag = jnp.einsum('bcijg,bcijgr,bcjgrp->bcigrp', cb, decay * dtc[:, :, None], xc)
    w_end = jnp.exp(acs[:, :, -1:] - acs) * dtc
    states = jnp.einsum('bcjgn,bcjgr,bcjgrp->bcgrpn', bc, w_end, xc)
    chunk_decay = jnp.exp(acs[:, :, -1])

    def step(hc, inp):
        st, dc = inp
        return hc * dc[..., None, None] + st, hc

    h_last, h_before = lax.scan(step, h0.astype(F32).reshape(b, g, r, p, n),
                                (jnp.moveaxis(states, 1, 0), jnp.moveaxis(chunk_decay, 1, 0)))
    h_before = jnp.moveaxis(h_before, 0, 1)
    y_off = jnp.einsum('bcign,bcgrpn,bcigr->bcigrp', cc, h_before, jnp.exp(acs))
    y = (y_diag + y_off).reshape(b, L, nh, p).astype(x.dtype)
    return y, h_last.reshape(b, nh, p, n)


def _ssm_core(zxd, h0, conv_w, conv_b, dt_bias, a_log, d_skip, norm_w):
    b, L, _ = zxd.shape
    z, xbc, dt_raw = jnp.split(zxd, _SSM_SPLITS, axis=-1)
    xbc = jax.nn.silu(_conv_centred(xbc, conv_w, conv_b))
    xs, bm, cm = jnp.split(xbc, [SSM_D_INNER, SSM_D_INNER + SSM_GROUPS * SSM_STATE], axis=-1)
    xs = xs.reshape(b, L, SSM_HEADS, SSM_HEAD_DIM)
    bm = bm.reshape(b, L, SSM_GROUPS, SSM_STATE)
    cm = cm.reshape(b, L, SSM_GROUPS, SSM_STATE)
    dt = jax.nn.softplus(dt_raw.reshape(b, L, 2, SSM_HEADS).astype(F32) + dt_bias.astype(F32))
    a = -jnp.exp(a_log.astype(F32))
    flip = lambda t: jnp.flip(t, axis=1)
    y_f, h_f = _ssd_scan(xs, dt[:, :, 0], a[0], bm, cm, h0[:, 0])
    y_b, h_b = _ssd_scan(flip(xs), flip(dt[:, :, 1]), a[1], flip(bm), flip(cm), h0[:, 1])
    y = y_f + d_skip[0][:, None] * xs + flip(y_b) + d_skip[1][:, None] * xs
    y = y.reshape(b, L, SSM_D_INNER) * jax.nn.silu(z)
    y = _rms_norm(y.reshape(b, L, SSM_GROUPS, -1), norm_w.reshape(SSM_GROUPS, -1), eps=1e-5)
    return y.reshape(b, L, SSM_D_INNER), jnp.stack([h_f, h_b], axis=1)


def _modulation(cvec, w, b):
    m = jax.nn.silu(cvec) @ w + b
    return [t[:, None, :] for t in jnp.split(m, 6, axis=-1)]


def _modulate(x, shift, scale):
    return x * (1.0 + scale) + shift


def _residual(x, sub, gate, g, b):
    return _layer_norm(DEEPNORM_ALPHA * x + gate * sub, g, b)


def kernel(x_prompt, x_sample, c, cache_na_k, cache_na_v, cache_mla_ckv, cache_mla_kpe, state_ssm,
           c_ctx, w_ada, b_ada, ln_g, ln_b, attn_w_in, na_rpb, mla_q_norm, mla_w_uq, mla_kv_norm,
           mla_w_ukv, attn_w_out, ssm_w_in, ssm_conv_w, ssm_conv_b, ssm_dt_bias, ssm_a_log, ssm_d,
           ssm_norm, ssm_w_out, moe_w_router, moe_b_router, moe_w_gu, moe_b_gu, moe_w_down, moe_b_down):
    xp, xs = x_prompt, x_sample
    bp, lp, D = xp.shape
    bs, ls, _ = xs.shape
    tp, ts = bp * lp, bs * ls
    new_k, new_v, new_ckv, new_kpe, new_ssm = [], [], [], [], []

    def both(fp, fs):
        return jnp.concatenate([fp.reshape(tp, -1), fs.reshape(ts, -1)], 0)

    def split(t):
        return t[:tp].reshape(bp, lp, -1), t[tp:].reshape(bs, ls, -1)

    for l in range(DEPTH):
        mc = _modulation(c_ctx[None, :], w_ada[l], b_ada[l])
        ml = _modulation(c, w_ada[l], b_ada[l])
        h = both(_modulate(xp, mc[0], mc[1]), _modulate(xs, ml[0], ml[1])).astype(BF16)
        if l % 2 == 0:
            i = l // 2
            w_in = _pad_cols(attn_w_in[i].astype(BF16), 512)
            proj = _matmul(h, w_in)
            na_q, na_k, na_v, cq, ckv, kpe = jnp.split(proj[:, :attn_w_in.shape[-1]], _ATTN_SPLITS, axis=-1)
            q_all = _matmul(_rms_norm(cq, mla_q_norm[i]).astype(BF16), mla_w_uq[i].astype(BF16))
            ckv = _rms_norm(ckv, mla_kv_norm[i])
            c_ckv = cache_mla_ckv[:, i]
            ckv_ext = jnp.concatenate([ckv, c_ckv.reshape(-1, MLA_KV_LORA)], 0)
            kv_all = _matmul(ckv_ext.astype(BF16), mla_w_ukv[i].astype(BF16))

            def heads(t, nb_, l_, hd):
                return t.reshape(nb_, l_, -1, hd)

            hq = MLA_NOPE + MLA_ROPE_DIM
            hkv = MLA_NOPE + MLA_V
            qp_ = heads(q_all[:tp], bp, lp, hq)
            kvp = heads(kv_all[:tp], bp, lp, hkv)
            k_c = heads(na_k[:tp], bp, lp, NA_HEAD_DIM)
            v_c = heads(na_v[:tp], bp, lp, NA_HEAD_DIM)
            o_na_p = _dense_attend(heads(na_q[:tp], bp, lp, NA_HEAD_DIM), k_c, v_c)
            ckv_p = ckv[:tp].reshape(bp, lp, -1)
            kpe_p = kpe[:tp].reshape(bp, lp, -1)
            o_mla_p = _mla_attend(qp_[..., :MLA_NOPE], qp_[..., MLA_NOPE:], kvp[..., :MLA_NOPE], kpe_p,
                                  kvp[..., MLA_NOPE:])
            qs_ = heads(q_all[tp:], bs, ls, hq)
            kvs = heads(kv_all[tp:tp + ts], bs, ls, hkv)
            kvc = heads(kv_all[tp + ts:], bs, c_ckv.shape[1], hkv)
            o_na_s = _na_latent(heads(na_q[tp:], bs, ls, NA_HEAD_DIM), heads(na_k[tp:], bs, ls, NA_HEAD_DIM),
                                heads(na_v[tp:], bs, ls, NA_HEAD_DIM), cache_na_k[:, i], cache_na_v[:, i],
                                na_rpb[i])
            kpe_s = kpe[tp:].reshape(bs, ls, -1)
            o_mla_s = _mla_attend(qs_[..., :MLA_NOPE], _rope_2d(qs_[..., MLA_NOPE:]),
                                  jnp.concatenate([kvs[..., :MLA_NOPE], kvc[..., :MLA_NOPE]], 1),
                                  jnp.concatenate([_rope_2d(kpe_s), cache_mla_kpe[:, i]], 1),
                                  jnp.concatenate([kvs[..., MLA_NOPE:], kvc[..., MLA_NOPE:]], 1))
            o_cat = jnp.concatenate([both(o_na_p, o_na_s), both(o_mla_p, o_mla_s)], -1)
            y = _matmul(o_cat.astype(BF16), attn_w_out[i].astype(BF16))
            new_k.append(k_c)
            new_v.append(v_c)
            new_ckv.append(ckv_p)
            new_kpe.append(kpe_p)
        else:
            j = l // 2
            w_in = _pad_cols(ssm_w_in[j].astype(BF16), 512)
            zxd = _matmul(h, w_in)[:, :ssm_w_in.shape[-1]]
            params = (ssm_conv_w[j], ssm_conv_b[j], ssm_dt_bias[j], ssm_a_log[j], ssm_d[j], ssm_norm[j])
            zero_state = jnp.zeros((bp, 2, SSM_HEADS, SSM_HEAD_DIM, SSM_STATE), xp.dtype)
            yp_, st = _ssm_core(zxd[:tp].reshape(bp, lp, -1), zero_state, *params)
            ys_, _ = _ssm_core(zxd[tp:].reshape(bs, ls, -1), state_ssm[:, j], *params)
            y = _matmul(both(yp_, ys_).astype(BF16), ssm_w_out[j].astype(BF16))
            new_ssm.append(st)
        yp, ys = split(y)
        xp = _residual(xp, yp, mc[2], ln_g[l, 0], ln_b[l, 0])
        xs = _residual(xs, ys, ml[2], ln_g[l, 0], ln_b[l, 0])
        hm = both(_modulate(xp, mc[3], mc[4]), _modulate(xs, ml[3], ml[4]))
        ym = _moe(hm, moe_w_router[l], moe_b_router[l], moe_w_gu[l].astype(BF16), moe_b_gu[l],
                  moe_w_down[l].astype(BF16), moe_b_down[l])
        ymp, yms = split(ym)
        xp = _residual(xp, ymp, mc[5], ln_g[l, 1], ln_b[l, 1])
        xs = _residual(xs, yms, ml[5], ln_g[l, 1], ln_b[l, 1])
    return (xp, xs, jnp.stack(new_k, 1), jnp.stack(new_v, 1), jnp.stack(new_ckv, 1),
            jnp.stack(new_kpe, 1), jnp.stack(new_ssm, 1))
```

```python
import functools
import math

import jax
import jax.numpy as jnp
import numpy as np
from jax import lax
from jax.experimental import pallas as pl
from jax.experimental.pallas import tpu as pltpu

D_MODEL = 2048
DEPTH = 4
GRID_W = 64

NA_HEADS = 8
NA_HEAD_DIM = 128
NA_WR = 8
NA_WC = 16

MLA_HEADS = 8
MLA_Q_LORA = 768
MLA_KV_LORA = 512
MLA_NOPE = 128
MLA_ROPE_DIM = 64
MLA_V = 128
MLA_QK_PAD = 256
ROPE_THETA = 10000.0

NA_QKV = NA_HEADS * NA_HEAD_DIM

SSM_D_INNER = 2 * D_MODEL
SSM_HEAD_DIM = 64
SSM_HEADS = SSM_D_INNER // SSM_HEAD_DIM
SSM_GROUPS = 8
SSM_HPG = SSM_HEADS // SSM_GROUPS
SSM_GROUP_W = SSM_HPG * SSM_HEAD_DIM
SSM_STATE = 128
SSM_CONV_W = 5
SSM_CHUNK = 128
SSM_CONV_CH = SSM_D_INNER + 2 * SSM_GROUPS * SSM_STATE

N_EXPERTS = 32
TOP_K = 4
D_FF = D_MODEL // 2
SWIGLU_LIMIT = 7.0
SWIGLU_ALPHA = 1.702

DEEPNORM_ALPHA = (2.0 * DEPTH) ** 0.25

V7X_VMEM_LIMIT_BYTES = 48 * 1024 * 1024
V7X_LANES = 128
MM_TILE_M = 1024
MM_TILE_N = 1024
MOE_TILE_M = 512
CAST_TILE_K = 512
ATTN_TILE_Q = 256
NA_ROWS_PER_STEP = 2
CONV_TILE_C = 512
CONV_HALO = 8

BF16 = jnp.bfloat16
F32 = jnp.float32
_NT = (((1,), (1,)), ((), ()))


def _params(*sem):
    return pltpu.CompilerParams(dimension_semantics=sem, vmem_limit_bytes=V7X_VMEM_LIMIT_BYTES)


def _mm_kernel(x_ref, w_ref, o_ref):
    o_ref[...] = jnp.dot(x_ref[...], w_ref[...], preferred_element_type=F32).astype(o_ref.dtype)


def _pick_tile(n, pref):
    for t in (pref, 768, 512, 384, 256, 128):
        if t <= pref and n % t == 0:
            return t
    return n


def _matmul(x, w, out_dtype=F32):
    m, k = x.shape
    n = w.shape[1]
    tm = _pick_tile(m, MM_TILE_M)
    tn = _pick_tile(n, MM_TILE_N)
    return pl.pallas_call(
        _mm_kernel,
        grid=(m // tm, n // tn),
        in_specs=[pl.BlockSpec((tm, k), lambda i, j: (i, 0)),
                  pl.BlockSpec((k, tn), lambda i, j: (0, j))],
        out_specs=pl.BlockSpec((tm, tn), lambda i, j: (i, j)),
        out_shape=jax.ShapeDtypeStruct((m, n), out_dtype),
        compiler_params=_params("parallel", "parallel"),
    )(x.astype(BF16), w.astype(BF16))


def _cast_kernel(x_ref, o_ref):
    o_ref[...] = x_ref[...].astype(o_ref.dtype)


def _cast_bf16(w):
    e, k, n = w.shape
    tk = _pick_tile(k, CAST_TILE_K)
    return pl.pallas_call(
        _cast_kernel,
        grid=(e, k // tk),
        in_specs=[pl.BlockSpec((1, tk, n), lambda i, j: (i, j, 0))],
        out_specs=pl.BlockSpec((1, tk, n), lambda i, j: (i, j, 0)),
        out_shape=jax.ShapeDtypeStruct(w.shape, BF16),
        compiler_params=_params("parallel", "parallel"),
    )(w)


def _router_kernel(x_ref, w_ref, b_ref, o_ref):
    o_ref[...] = jnp.dot(x_ref[...], w_ref[...], preferred_element_type=F32,
                         precision=lax.Precision.HIGHEST) + b_ref[...]


def _router_logits(x, w, b):
    m, k = x.shape
    n = w.shape[1]
    tm = 512
    return pl.pallas_call(
        _router_kernel,
        grid=(m // tm,),
        in_specs=[pl.BlockSpec((tm, k), lambda i: (i, 0)),
                  pl.BlockSpec((k, n), lambda i: (0, 0)),
                  pl.BlockSpec((1, n), lambda i: (0, 0))],
        out_specs=pl.BlockSpec((tm, n), lambda i: (i, 0)),
        out_shape=jax.ShapeDtypeStruct((m, n), F32),
        compiler_params=_params("parallel"),
    )(x, w, b.reshape(1, n))


def _moe_ffn_kernel(be_ref, nused_ref, x_ref, wgu_ref, bgu_ref, wd_ref, bd_ref, o_ref):
    @pl.when(pl.program_id(0) < nused_ref[0])
    def _():
        x = x_ref[...]
        gu = jnp.dot(x, wgu_ref[0], preferred_element_type=F32) + bgu_ref[0]
        x_glu = jnp.minimum(gu[:, :D_FF], SWIGLU_LIMIT)
        x_lin = jnp.clip(gu[:, D_FF:], -SWIGLU_LIMIT, SWIGLU_LIMIT)
        act = x_glu * jax.nn.sigmoid(SWIGLU_ALPHA * x_glu) * (x_lin + 1.0)
        y = jnp.dot(act.astype(BF16), wd_ref[0], preferred_element_type=F32) + bd_ref[0]
        o_ref[...] = y.astype(o_ref.dtype)

    @pl.when(pl.program_id(0) >= nused_ref[0])
    def _():
        o_ref[...] = jnp.zeros_like(o_ref)


def _moe_ffn(xb, block_e, n_used, w_gu, b_gu, w_down, b_down):
    rows, d = xb.shape
    nb = rows // MOE_TILE_M
    grid_spec = pltpu.PrefetchScalarGridSpec(
        num_scalar_prefetch=2,
        grid=(nb,),
        in_specs=[pl.BlockSpec((MOE_TILE_M, d), lambda i, be, nu: (i, 0)),
                  pl.BlockSpec((1, d, 2 * D_FF), lambda i, be, nu: (be[i], 0, 0)),
                  pl.BlockSpec((1, 1, 2 * D_FF), lambda i, be, nu: (be[i], 0, 0)),
                  pl.BlockSpec((1, D_FF, d), lambda i, be, nu: (be[i], 0, 0)),
                  pl.BlockSpec((1, 1, d), lambda i, be, nu: (be[i], 0, 0))],
        out_specs=pl.BlockSpec((MOE_TILE_M, d), lambda i, be, nu: (i, 0)),
    )
    return pl.pallas_call(
        _moe_ffn_kernel,
        grid_spec=grid_spec,
        out_shape=jax.ShapeDtypeStruct((rows, d), BF16),
        compiler_params=_params("arbitrary"),
    )(block_e, n_used, xb, w_gu, b_gu.reshape(N_EXPERTS, 1, 2 * D_FF), w_down,
      b_down.reshape(N_EXPERTS, 1, d))


def _moe(x, w_router, b_router, w_gu, b_gu, w_down, b_down):
    T, d = x.shape
    tm = MOE_TILE_M
    logits = _router_logits(x, w_router, b_router)
    top_v, top_e = lax.top_k(logits, TOP_K)
    gates = jax.nn.softmax(top_v, -1)
    n = T * TOP_K
    flat_e = top_e.reshape(-1)
    onehot = (flat_e[:, None] == jnp.arange(N_EXPERTS, dtype=jnp.int32)[None, :]).astype(jnp.int32)
    csum = jnp.cumsum(onehot, axis=0)
    counts = csum[-1]
    rank = jnp.sum((csum - 1) * onehot, axis=1)
    padded = (counts + tm - 1) // tm * tm
    pad_end = jnp.cumsum(padded)
    pad_start = pad_end - padded
    start = jnp.cumsum(counts) - counts
    pos = pad_start[flat_e] + rank
    nb = n // tm + N_EXPERTS
    order = jnp.argsort(flat_e)
    n_used = (pad_end[-1] // tm).astype(jnp.int32)
    blk = jnp.arange(nb, dtype=jnp.int32)
    block_e = jnp.minimum(jnp.searchsorted(pad_end, blk * tm, side='right'), N_EXPERTS - 1).astype(jnp.int32)
    last_e = block_e[jnp.maximum(n_used - 1, 0)]
    block_e = jnp.where(blk < n_used, block_e, last_e)
    r = jnp.arange(nb * tm, dtype=jnp.int32)
    row_e = jnp.repeat(block_e, tm)
    off = r - pad_start[row_e]
    valid = (off < counts[row_e]) & (r < pad_end[-1])
    src = jnp.clip(start[row_e] + off, 0, n - 1)
    row_tok = jnp.where(valid, order[src] // TOP_K, 0)
    xb = x.astype(BF16)[row_tok]
    yb = _moe_ffn(xb, block_e, n_used.reshape(1), w_gu, b_gu, w_down, b_down)
    y = yb[pos].reshape(T, TOP_K, d) * gates[..., None]
    return jnp.sum(y, axis=1)


def _softmax_pv(s_list, v_list):
    m = functools.reduce(jnp.maximum, [jnp.max(s, -1, keepdims=True) for s in s_list])
    p_list = [jnp.exp(s - m) for s in s_list]
    denom = functools.reduce(jnp.add, [jnp.sum(p, -1, keepdims=True) for p in p_list])
    o = functools.reduce(jnp.add, [jnp.dot(p.astype(BF16), v, preferred_element_type=F32)
                                   for p, v in zip(p_list, v_list)])
    return o / denom


def _mha_kernel(q_ref, k_ref, v_ref, o_ref, *, heads, dk, dv, scale):
    for h in range(heads):
        q = q_ref[0, :, h * dk:(h + 1) * dk]
        k = k_ref[0, :, h * dk:(h + 1) * dk]
        s = lax.dot_general(q, k, _NT, preferred_element_type=F32) * scale
        o = _softmax_pv([s], [v_ref[0, :, h * dv:(h + 1) * dv]])
        o_ref[0, :, h * dv:(h + 1) * dv] = o.astype(o_ref.dtype)


def _mha(q, k, v, heads, scale):
    b, lq, hdk = q.shape
    lk = k.shape[1]
    dk = hdk // heads
    dv = v.shape[2] // heads
    tq = min(ATTN_TILE_Q, lq)
    return pl.pallas_call(
        functools.partial(_mha_kernel, heads=heads, dk=dk, dv=dv, scale=scale),
        grid=(b, lq // tq),
        in_specs=[pl.BlockSpec((1, tq, hdk), lambda i, j: (i, j, 0)),
                  pl.BlockSpec((1, lk, hdk), lambda i, j: (i, 0, 0)),
                  pl.BlockSpec((1, lk, heads * dv), lambda i, j: (i, 0, 0))],
        out_specs=pl.BlockSpec((1, tq, heads * dv), lambda i, j: (i, j, 0)),
        out_shape=jax.ShapeDtypeStruct((b, lq, heads * dv), BF16),
        compiler_params=_params("parallel", "arbitrary"),
    )(q, k, v)


def _na_kernel(q_ref, k_ref, v_ref, kc_ref, vc_ref, bias_ref, o_ref, *, heads, d, scale, grid_rows, win):
    rows_per_step = bias_ref.shape[0]
    for rr in range(rows_per_step):
        r = pl.program_id(1) * rows_per_step + rr
        rs = jnp.clip(r - NA_WR // 2, 0, grid_rows - win)
        start = pl.multiple_of(rs * GRID_W, GRID_W)
        for h in range(heads):
            hs = slice(h * d, (h + 1) * d)
            q = q_ref[0, rr * GRID_W:(rr + 1) * GRID_W, hs]
            k_loc = k_ref[0, pl.ds(start, win * GRID_W), hs]
            v_loc = v_ref[0, pl.ds(start, win * GRID_W), hs]
            s_loc = lax.dot_general(q, k_loc, _NT, preferred_element_type=F32) * scale + bias_ref[rr, h]
            s_ctx = lax.dot_general(q, kc_ref[0, :, hs], _NT, preferred_element_type=F32) * scale
            o = _softmax_pv([s_loc, s_ctx], [v_loc, vc_ref[0, :, hs]])
            o_ref[0, rr * GRID_W:(rr + 1) * GRID_W, hs] = o.astype(o_ref.dtype)


def _na_bias(rpb, grid_rows, win):
    qc = np.arange(GRID_W)
    kc = np.arange(GRID_W)
    qcs = np.clip(qc - NA_WC // 2, 0, GRID_W - NA_WC)
    col_ok = (kc[None, :] >= qcs[:, None]) & (kc[None, :] < qcs[:, None] + NA_WC)
    col_idx = np.clip(kc[None, :] - qc[:, None] + NA_WC - 1, 0, 2 * NA_WC - 2)
    r = np.arange(grid_rows)
    rs = np.clip(r - NA_WR // 2, 0, grid_rows - win)
    row_idx = rs[:, None] + np.arange(win)[None, :] - r[:, None] + NA_WR - 1
    bias = rpb[:, row_idx][:, :, :, col_idx]
    bias = jnp.where(jnp.asarray(col_ok)[None, None, None], bias.astype(F32), -1e30)
    bias = jnp.transpose(bias, (1, 0, 3, 2, 4))
    return bias.reshape(grid_rows, rpb.shape[0], GRID_W, win * GRID_W)


def _na_latent(q, k, v, k_ctx, v_ctx, rpb):
    b, L, hd = q.shape
    heads = rpb.shape[0]
    d = hd // heads
    lc = k_ctx.shape[1]
    grid_rows = L // GRID_W
    win = min(NA_WR, grid_rows)
    rps = NA_ROWS_PER_STEP
    bias = _na_bias(rpb, grid_rows, win)
    return pl.pallas_call(
        functools.partial(_na_kernel, heads=heads, d=d, scale=d ** -0.5, grid_rows=grid_rows, win=win),
        grid=(b, grid_rows // rps),
        in_specs=[pl.BlockSpec((1, rps * GRID_W, hd), lambda i, j: (i, j, 0)),
                  pl.BlockSpec((1, L, hd), lambda i, j: (i, 0, 0)),
                  pl.BlockSpec((1, L, hd), lambda i, j: (i, 0, 0)),
                  pl.BlockSpec((1, lc, hd), lambda i, j: (i, 0, 0)),
                  pl.BlockSpec((1, lc, hd), lambda i, j: (i, 0, 0)),
                  pl.BlockSpec((rps, heads, GRID_W, win * GRID_W), lambda i, j: (j, 0, 0, 0))],
        out_specs=pl.BlockSpec((1, rps * GRID_W, hd), lambda i, j: (i, j, 0)),
        out_shape=jax.ShapeDtypeStruct((b, L, hd), BF16),
        compiler_params=_params("parallel", "arbitrary"),
    )(q, k, v, k_ctx, v_ctx, bias)


def _conv_silu_kernel(x_ref, w_ref, b_ref, o_ref, pad_ref):
    L = x_ref.shape[1]
    zeros = jnp.zeros((CONV_HALO, pad_ref.shape[1]), F32)
    pad_ref[0:CONV_HALO, :] = zeros
    pad_ref[CONV_HALO + L:, :] = zeros
    pad_ref[CONV_HALO:CONV_HALO + L, :] = x_ref[0]
    half = SSM_CONV_W // 2
    acc = b_ref[...] + w_ref[0:1, :] * pad_ref[CONV_HALO - half:CONV_HALO - half + L, :]
    for t in range(1, SSM_CONV_W):
        off = CONV_HALO - half + t
        acc = acc + w_ref[t:t + 1, :] * pad_ref[off:off + L, :]
    o_ref[0] = acc * jax.nn.sigmoid(acc)


def _conv_silu(x, w, b):
    bsz, L, C = x.shape
    tc = CONV_TILE_C
    return pl.pallas_call(
        _conv_silu_kernel,
        grid=(bsz, C // tc),
        in_specs=[pl.BlockSpec((1, L, tc), lambda i, j: (i, 0, j)),
                  pl.BlockSpec((SSM_CONV_W, tc), lambda i, j: (0, j)),
                  pl.BlockSpec((1, tc), lambda i, j: (0, j))],
        out_specs=pl.BlockSpec((1, L, tc), lambda i, j: (i, 0, j)),
        out_shape=jax.ShapeDtypeStruct((bsz, L, C), F32),
        scratch_shapes=[pltpu.VMEM((L + 2 * CONV_HALO, tc), F32)],
        compiler_params=_params("parallel", "parallel"),
    )(x, w, b.reshape(1, C))


def _ssd_kernel(xf_ref, xb_ref, bf_ref, bb_ref, cf_ref, cb_ref, dtf_ref, dtb_ref, dttf_ref, dttb_ref,
                a_ref, at_ref, dsk_ref, h0_ref, yf_ref, yb_ref, hl_ref):
    c = pl.program_id(2)
    Q = SSM_CHUNK
    P = SSM_HEAD_DIM

    @pl.when(c == 0)
    def _():
        hl_ref[...] = h0_ref[...]

    row = lax.broadcasted_iota(jnp.int32, (Q, Q), 0)
    col = lax.broadcasted_iota(jnp.int32, (Q, Q), 1)
    lower = (col <= row)
    upper = (col >= row)
    lower_f = lower.astype(F32)
    upper_f = upper.astype(F32)
    lane = lax.broadcasted_iota(jnp.int32, (Q, V7X_LANES), 1)
    lo_half = lane < P
    hi = lax.Precision.HIGHEST

    dirs = ((0, xf_ref, bf_ref, cf_ref, dtf_ref, dttf_ref, yf_ref, lower, lower_f, upper_f, Q - 1),
            (1, xb_ref, bb_ref, cb_ref, dtb_ref, dttb_ref, yb_ref, upper, upper_f, lower_f, 0))
    for (d, x_ref, b_ref, c_ref, dt_ref, dtt_ref, y_ref, valid, cum_mat, cum_mat_t, last) in dirs:
        dt = dt_ref[0, 0, 0]
        dtt = dtt_ref[0, 0, 0]
        a_row = a_ref[0, d]
        a_col = at_ref[0, d]
        acs = jnp.dot(cum_mat, dt * a_row, preferred_element_type=F32, precision=hi)
        acst = jnp.dot(dtt * a_col, cum_mat_t, preferred_element_type=F32, precision=hi)
        total = acs[last:last + 1, :]
        w_end = jnp.exp(total - acs) * dt
        e_acs = jnp.exp(acs)
        e_tot_t = jnp.exp(acst[:, last:last + 1])
        bmat = b_ref[0].astype(BF16)
        cmat = c_ref[0].astype(BF16)
        cbm = lax.dot_general(cmat, bmat, _NT, preferred_element_type=F32)
        h_prev = hl_ref[0, d]
        y_off = lax.dot_general(cmat, h_prev.astype(BF16), _NT, preferred_element_type=F32)
        xw_slabs = []
        for s in range(SSM_HPG // 2):
            h0, h1 = 2 * s, 2 * s + 1
            x_slab = x_ref[0, :, s * V7X_LANES:(s + 1) * V7X_LANES]
            y_slab = None
            for hh, keep in ((h0, lo_half), (h1, ~lo_half)):
                seg = acs[:, hh:hh + 1] - acst[hh:hh + 1, :]
                decay = jnp.where(valid, jnp.exp(jnp.where(valid, seg, 0.0)), 0.0)
                m = (cbm * decay * dtt[hh:hh + 1, :]).astype(BF16)
                part = jnp.dot(m, jnp.where(keep, x_slab, 0.0).astype(BF16), preferred_element_type=F32)
                y_slab = part if y_slab is None else y_slab + part
            scale_off = jnp.where(lo_half, e_acs[:, h0:h0 + 1], e_acs[:, h1:h1 + 1])
            skip = dsk_ref[0, d, :, s * V7X_LANES:(s + 1) * V7X_LANES]
            y_slab = y_slab + y_off[:, s * V7X_LANES:(s + 1) * V7X_LANES] * scale_off + skip * x_slab
            y_ref[0, :, s * V7X_LANES:(s + 1) * V7X_LANES] = y_slab
            w_slab = jnp.where(lo_half, w_end[:, h0:h0 + 1], w_end[:, h1:h1 + 1])
            xw_slabs.append((x_slab * w_slab).astype(BF16))
        xw = jnp.concatenate(xw_slabs, axis=-1)
        states = lax.dot_general(xw, bmat, (((0,), (0,)), ((), ())), preferred_element_type=F32)
        for hh in range(SSM_HPG):
            rows = slice(hh * P, (hh + 1) * P)
            hl_ref[0, d, rows, :] = h_prev[rows, :] * e_tot_t[hh:hh + 1, :] + states[rows, :]


def _ssd(xbc, dt, a, d_skip, h0):
    bsz, L, _ = xbc.shape
    nc = L // SSM_CHUNK
    G, HPG, Q, N = SSM_GROUPS, SSM_HPG, SSM_CHUNK, SSM_STATE
    gw = SSM_GROUP_W
    dt_g = jnp.transpose(dt.reshape(bsz, L, 2, G, HPG), (2, 0, 3, 1, 4))
    dtt_g = jnp.transpose(dt_g, (0, 1, 2, 4, 3))
    a_g = jnp.transpose(a.reshape(2, G, 1, HPG), (1, 0, 2, 3))
    at_g = jnp.transpose(a_g, (0, 1, 3, 2))
    dsk = jnp.repeat(d_skip.reshape(2, G, HPG), SSM_HEAD_DIM, axis=-1)
    dsk = jnp.transpose(dsk, (1, 0, 2)).reshape(G, 2, 1, gw)
    nb_x = SSM_D_INNER // N
    fwd = lambda c: c
    bwd = lambda c: nc - 1 - c
    x_spec = lambda f: pl.BlockSpec((1, Q, gw), lambda b, g, c: (b, f(c), g))
    b_spec = lambda f: pl.BlockSpec((1, Q, N), lambda b, g, c: (b, f(c), nb_x + g))
    c_spec = lambda f: pl.BlockSpec((1, Q, N), lambda b, g, c: (b, f(c), nb_x + G + g))
    dt_spec = lambda d, f: pl.BlockSpec((1, 1, 1, Q, HPG), lambda b, g, c: (d, b, g, f(c), 0))
    dtt_spec = lambda d, f: pl.BlockSpec((1, 1, 1, HPG, Q), lambda b, g, c: (d, b, g, 0, f(c)))
    y_spec = lambda f: pl.BlockSpec((1, Q, gw), lambda b, g, c: (b, f(c), g))
    h_spec = pl.BlockSpec((1, 2, gw, N), lambda b, g, c: (b, 0, g, 0))
    return pl.pallas_call(
        _ssd_kernel,
        grid=(bsz, G, nc),
        in_specs=[x_spec(fwd), x_spec(bwd), b_spec(fwd), b_spec(bwd), c_spec(fwd), c_spec(bwd),
                  dt_spec(0, fwd), dt_spec(1, bwd), dtt_spec(0, fwd), dtt_spec(1, bwd),
                  pl.BlockSpec((1, 2, 1, HPG), lambda b, g, c: (g, 0, 0, 0)),
                  pl.BlockSpec((1, 2, HPG, 1), lambda b, g, c: (g, 0, 0, 0)),
                  pl.BlockSpec((1, 2, 1, gw), lambda b, g, c: (g, 0, 0, 0)),
                  h_spec],
        out_specs=[y_spec(fwd), y_spec(bwd), h_spec],
        out_shape=[jax.ShapeDtypeStruct((bsz, L, SSM_D_INNER), F32),
                   jax.ShapeDtypeStruct((bsz, L, SSM_D_INNER), F32),
                   jax.ShapeDtypeStruct(h0.shape, F32)],
        compiler_params=_params("parallel", "parallel", "arbitrary"),
    )(xbc, xbc, xbc, xbc, xbc, xbc, dt_g, dt_g, dtt_g, dtt_g, a_g, at_g, dsk, h0)


def _layer_norm(x, g, b, eps=1e-5):
    mu = jnp.mean(x, -1, keepdims=True)
    var = jnp.mean(jnp.square(x - mu), -1, keepdims=True)
    return ((x - mu) * lax.rsqrt(var + eps)) * g + b


def _rms_norm(x, g, eps=1e-6):
    return (x * lax.rsqrt(jnp.mean(jnp.square(x), -1, keepdims=True) + eps)) * g


def _rope_2d(x):
    L = x.shape[1]
    half = x.shape[-1] // 2
    quarter = half // 2
    t = jnp.arange(L)
    inv = ROPE_THETA ** (-jnp.arange(quarter, dtype=F32) / quarter)
    ang_r = (t // GRID_W).astype(F32)[:, None] * inv
    ang_c = (t % GRID_W).astype(F32)[:, None] * inv
    bshape = (L,) + (1,) * (x.ndim - 3) + (quarter,)

    def rot(v, ang):
        cos = jnp.cos(ang).reshape(bshape).astype(v.dtype)
        sin = jnp.sin(ang).reshape(bshape).astype(v.dtype)
        v1, v2 = v[..., :quarter], v[..., quarter:]
        return jnp.concatenate([v1 * cos - v2 * sin, v2 * cos + v1 * sin], -1)

    return jnp.concatenate([rot(x[..., :half], ang_r), rot(x[..., half:], ang_c)], -1)


def _mla_pack_q(q_nope, q_pe):
    b, L, h, _ = q_nope.shape
    pad = jnp.zeros((b, L, h, MLA_QK_PAD - MLA_NOPE - MLA_ROPE_DIM), BF16)
    return jnp.concatenate([q_nope.astype(BF16), q_pe.astype(BF16), pad], -1).reshape(b, L, h * MLA_QK_PAD)


def _mla_pack_k(k_nope, k_pe):
    b, L, h, _ = k_nope.shape
    kpe = jnp.broadcast_to(k_pe.astype(BF16)[:, :, None, :], (b, L, h, MLA_ROPE_DIM))
    pad = jnp.zeros((b, L, h, MLA_QK_PAD - MLA_NOPE - MLA_ROPE_DIM), BF16)
    return jnp.concatenate([k_nope.astype(BF16), kpe, pad], -1).reshape(b, L, h * MLA_QK_PAD)


def _ssm_mixer(h, bp, lp, bs, ls, h0_s, w_in, conv_w, conv_b, dt_bias, a_log, d_skip, norm_w, w_out):
    tp = bp * lp
    w_in = w_in.astype(BF16)
    z = _matmul(h, w_in[:, :SSM_D_INNER])
    xbc = _matmul(h, w_in[:, SSM_D_INNER:SSM_D_INNER + SSM_CONV_CH])
    dt_raw = _matmul(h, w_in[:, SSM_D_INNER + SSM_CONV_CH:])
    dt = jax.nn.softplus(dt_raw.reshape(-1, 2, SSM_HEADS) + dt_bias)
    a = -jnp.exp(a_log)
    ys = []
    st_p = None
    for (t0, bsz, L, h0) in ((0, bp, lp, None), (tp, bs, ls, h0_s)):
        n_tok = bsz * L
        xa = _conv_silu(xbc[t0:t0 + n_tok].reshape(bsz, L, SSM_CONV_CH), conv_w, conv_b)
        if h0 is None:
            h0 = jnp.zeros((bsz, 2, SSM_D_INNER, SSM_STATE), F32)
        else:
            h0 = h0.reshape(bsz, 2, SSM_D_INNER, SSM_STATE)
        y_f, y_b, h_last = _ssd(xa, dt[t0:t0 + n_tok].reshape(bsz, L, 2, SSM_HEADS), a, d_skip, h0)
        if st_p is None:
            st_p = h_last.reshape(bsz, 2, SSM_HEADS, SSM_HEAD_DIM, SSM_STATE)
        ys.append((y_f + y_b).reshape(n_tok, SSM_D_INNER))
    y = jnp.concatenate(ys, 0) * jax.nn.silu(z)
    y = _rms_norm(y.reshape(-1, SSM_GROUPS, SSM_GROUP_W), norm_w.reshape(SSM_GROUPS, -1), eps=1e-5)
    return _matmul(y.reshape(-1, SSM_D_INNER).astype(BF16), w_out), st_p


def _attn_mixer(h, bp, lp, bs, ls, c_k, c_v, c_ckv, c_kpe, w_in, rpb, q_norm, w_uq, kv_norm, w_ukv, w_out):
    tp, ts = bp * lp, bs * ls
    w_in = w_in.astype(BF16)
    o_qkv, o_cq, o_ckv = 3 * NA_QKV, 3 * NA_QKV + MLA_Q_LORA, 3 * NA_QKV + MLA_Q_LORA + MLA_KV_LORA
    qkv = _matmul(h, w_in[:, :o_qkv])
    cq = _matmul(h, w_in[:, o_qkv:o_cq])
    ckv = _rms_norm(_matmul(h, w_in[:, o_cq:o_ckv]), kv_norm)
    kpe = _matmul(h, jnp.pad(w_in[:, o_ckv:], ((0, 0), (0, V7X_LANES - MLA_ROPE_DIM))))[:, :MLA_ROPE_DIM]
    q_all = _matmul(_rms_norm(cq, q_norm).astype(BF16), w_uq)
    lc = c_ckv.shape[1]
    kv_all = _matmul(jnp.concatenate([ckv, c_ckv.reshape(-1, MLA_KV_LORA)], 0).astype(BF16), w_ukv)

    na_q, na_k, na_v = (qkv[:, j * NA_QKV:(j + 1) * NA_QKV] for j in range(3))
    qb, kb, vb = na_q.astype(BF16), na_k.astype(BF16), na_v.astype(BF16)
    hq = MLA_NOPE + MLA_ROPE_DIM
    hkv = MLA_NOPE + MLA_V
    mla_scale = hq ** -0.5

    o_na_p = _mha(qb[:tp].reshape(bp, lp, -1), kb[:tp].reshape(bp, lp, -1), vb[:tp].reshape(bp, lp, -1),
                  NA_HEADS, NA_HEAD_DIM ** -0.5)
    q_p = q_all[:tp].reshape(bp, lp, MLA_HEADS, hq)
    kv_p = kv_all[:tp].reshape(bp, lp, MLA_HEADS, hkv)
    o_mla_p = _mha(_mla_pack_q(q_p[..., :MLA_NOPE], q_p[..., MLA_NOPE:]),
                   _mla_pack_k(kv_p[..., :MLA_NOPE], kpe[:tp].reshape(bp, lp, -1)),
                   kv_p[..., MLA_NOPE:].astype(BF16).reshape(bp, lp, -1), MLA_HEADS, mla_scale)

    o_na_s = _na_latent(qb[tp:].reshape(bs, ls, -1), kb[tp:].reshape(bs, ls, -1), vb[tp:].reshape(bs, ls, -1),
                        c_k.astype(BF16).reshape(bs, -1, NA_QKV), c_v.astype(BF16).reshape(bs, -1, NA_QKV), rpb)
    q_s = q_all[tp:].reshape(bs, ls, MLA_HEADS, hq)
    kv_s = jnp.concatenate([kv_all[tp:tp + ts].reshape(bs, ls, MLA_HEADS, hkv),
                            kv_all[tp + ts:].reshape(bs, lc, MLA_HEADS, hkv)], 1)
    kpe_s = jnp.concatenate([_rope_2d(kpe[tp:].reshape(bs, ls, -1)), c_kpe], 1)
    o_mla_s = _mha(_mla_pack_q(q_s[..., :MLA_NOPE], _rope_2d(q_s[..., MLA_NOPE:])),
                   _mla_pack_k(kv_s[..., :MLA_NOPE], kpe_s),
                   kv_s[..., MLA_NOPE:].astype(BF16).reshape(bs, ls + lc, -1), MLA_HEADS, mla_scale)

    o_cat = jnp.concatenate([jnp.concatenate([o_na_p.reshape(tp, -1), o_na_s.reshape(ts, -1)], 0),
                             jnp.concatenate([o_mla_p.reshape(tp, -1), o_mla_s.reshape(ts, -1)], 0)], -1)
    y = _matmul(o_cat, w_out)
    new = (na_k[:tp].reshape(bp, lp, NA_HEADS, NA_HEAD_DIM), na_v[:tp].reshape(bp, lp, NA_HEADS, NA_HEAD_DIM),
           ckv[:tp].reshape(bp, lp, -1), kpe[:tp].reshape(bp, lp, -1))
    return y, new


def _modulation(cvec, w, b):
    m = jax.nn.silu(cvec) @ w + b
    return [t[:, None, :] for t in jnp.split(m, 6, axis=-1)]


def _modulate(x, shift, scale):
    return x * (1.0 + scale) + shift


def _residual(x, sub, gate, g, b):
    return _layer_norm(DEEPNORM_ALPHA * x + gate * sub, g, b)


def kernel(x_prompt, x_sample, c, cache_na_k, cache_na_v, cache_mla_ckv, cache_mla_kpe, state_ssm,
           c_ctx, w_ada, b_ada, ln_g, ln_b, attn_w_in, na_rpb, mla_q_norm, mla_w_uq, mla_kv_norm,
           mla_w_ukv, attn_w_out, ssm_w_in, ssm_conv_w, ssm_conv_b, ssm_dt_bias, ssm_a_log, ssm_d,
           ssm_norm, ssm_w_out, moe_w_router, moe_b_router, moe_w_gu, moe_b_gu, moe_w_down, moe_b_down):
    xp, xs = x_prompt, x_sample
    bp, lp, D = xp.shape
    bs, ls, _ = xs.shape
    tp = bp * lp
    new_k, new_v, new_ckv, new_kpe, new_ssm = [], [], [], [], []

    def both(fp, fs):
        return jnp.concatenate([fp.reshape(tp, -1), fs.reshape(bs * ls, -1)], 0)

    def split(t):
        return t[:tp].reshape(bp, lp, -1), t[tp:].reshape(bs, ls, -1)

    for l in range(DEPTH):
        mc = _modulation(c_ctx[None, :], w_ada[l], b_ada[l])
        ml = _modulation(c, w_ada[l], b_ada[l])
        h = both(_modulate(xp, mc[0], mc[1]), _modulate(xs, ml[0], ml[1])).astype(BF16)
        if l % 2 == 0:
            i = l // 2
            y, (k_c, v_c, ckv_c, kpe_c) = _attn_mixer(
                h, bp, lp, bs, ls, cache_na_k[:, i], cache_na_v[:, i], cache_mla_ckv[:, i], cache_mla_kpe[:, i],
                attn_w_in[i], na_rpb[i], mla_q_norm[i], mla_w_uq[i], mla_kv_norm[i], mla_w_ukv[i], attn_w_out[i])
            new_k.append(k_c)
            new_v.append(v_c)
            new_ckv.append(ckv_c)
            new_kpe.append(kpe_c)
        else:
            j = l // 2
            y, st = _ssm_mixer(h, bp, lp, bs, ls, state_ssm[:, j], ssm_w_in[j], ssm_conv_w[j], ssm_conv_b[j],
                               ssm_dt_bias[j], ssm_a_log[j], ssm_d[j], ssm_norm[j], ssm_w_out[j])
            new_ssm.append(st)
        yp, ys = split(y)
        xp = _residual(xp, yp, mc[2], ln_g[l, 0], ln_b[l, 0])
        xs = _residual(xs, ys, ml[2], ln_g[l, 0], ln_b[l, 0])
        hm = both(_modulate(xp, mc[3], mc[4]), _modulate(xs, ml[3], ml[4]))
        ym = _moe(hm, moe_w_router[l], moe_b_router[l], _cast_bf16(moe_w_gu[l]), moe_b_gu[l],
                  _cast_bf16(moe_w_down[l]), moe_b_down[l])
        ymp, yms = split(ym)
        xp = _residual(xp, ymp, mc[5], ln_g[l, 1], ln_b[l, 1])
        xs = _residual(xs, yms, ml[5], ln_g[l, 1], ln_b[l, 1])
    return (xp, xs, jnp.stack(new_k, 1), jnp.stack(new_v, 1), jnp.stack(new_ckv, 1),
            jnp.stack(new_kpe, 1), jnp.stack(new_ssm, 1))
```

```python
import functools
import math

import jax
import jax.numpy as jnp
import numpy as np
from jax import lax
from jax.experimental import pallas as pl
from jax.experimental.pallas import tpu as pltpu

D_MODEL = 2048
DEPTH = 4
GRID_W = 64

NA_HEADS = 8
NA_HEAD_DIM = 128
NA_WR = 8
NA_WC = 16

MLA_HEADS = 8
MLA_Q_LORA = 768
MLA_KV_LORA = 512
MLA_NOPE = 128
MLA_ROPE_DIM = 64
MLA_V = 128
MLA_QK_PAD = 256
MLA_LATENT_PAD = 1536
ROPE_THETA = 10000.0

NA_QKV = NA_HEADS * NA_HEAD_DIM

SSM_D_INNER = 2 * D_MODEL
SSM_HEAD_DIM = 64
SSM_HEADS = SSM_D_INNER // SSM_HEAD_DIM
SSM_GROUPS = 8
SSM_HPG = SSM_HEADS // SSM_GROUPS
SSM_GROUP_W = SSM_HPG * SSM_HEAD_DIM
SSM_STATE = 128
SSM_CONV_W = 5
SSM_CHUNK = 128
SSM_CONV_CH = SSM_D_INNER + 2 * SSM_GROUPS * SSM_STATE

N_EXPERTS = 32
TOP_K = 4
D_FF = D_MODEL // 2
SWIGLU_LIMIT = 7.0
SWIGLU_ALPHA = 1.702

DEEPNORM_ALPHA = (2.0 * DEPTH) ** 0.25
LN_EPS = 1e-5
SSM_NORM_EPS = 1e-5
MLA_NORM_EPS = 1e-6

V7X_VMEM_LIMIT_BYTES = 48 * 1024 * 1024
V7X_LANES = 128
V7X_SUBLANES = 8
ROW_TILE = 256
MM_TILE_M = 1024
MM_TILE_N = 1024
MOE_TILE_M = 512
CAST_TILE_K = 512
ATTN_TILE_Q = 256
NA_ROWS_PER_STEP = 2
CONV_TILE_C = 512
CONV_HALO = 8

BF16 = jnp.bfloat16
F32 = jnp.float32
_NT = (((1,), (1,)), ((), ()))


def _params(*sem):
    return pltpu.CompilerParams(dimension_semantics=sem, vmem_limit_bytes=V7X_VMEM_LIMIT_BYTES)


def _mm_kernel(x_ref, w_ref, o_ref):
    o_ref[...] = jnp.dot(x_ref[...], w_ref[...], preferred_element_type=F32).astype(o_ref.dtype)


def _pick_tile(n, pref):
    for t in (pref, 768, 512, 384, 256, 128):
        if t <= pref and n % t == 0:
            return t
    return n


def _matmul(x, w, out_dtype=F32):
    m, k = x.shape
    n = w.shape[1]
    tm = _pick_tile(m, MM_TILE_M)
    tn = _pick_tile(n, MM_TILE_N)
    return pl.pallas_call(
        _mm_kernel,
        grid=(m // tm, n // tn),
        in_specs=[pl.BlockSpec((tm, k), lambda i, j: (i, 0)),
                  pl.BlockSpec((k, tn), lambda i, j: (0, j))],
        out_specs=pl.BlockSpec((tm, tn), lambda i, j: (i, j)),
        out_shape=jax.ShapeDtypeStruct((m, n), out_dtype),
        compiler_params=_params("parallel", "parallel"),
    )(x.astype(BF16), w.astype(BF16))


def _cast_kernel(x_ref, o_ref):
    o_ref[...] = x_ref[0].astype(o_ref.dtype)


def _cast_bf16(w, l):
    _, e, k, n = w.shape
    tk = _pick_tile(k, CAST_TILE_K)
    return pl.pallas_call(
        _cast_kernel,
        grid=(e, k // tk),
        in_specs=[pl.BlockSpec((1, 1, tk, n), lambda i, j: (l, i, j, 0))],
        out_specs=pl.BlockSpec((1, tk, n), lambda i, j: (i, j, 0)),
        out_shape=jax.ShapeDtypeStruct((e, k, n), BF16),
        compiler_params=_params("parallel", "parallel"),
    )(w)


def _req_of_tile(i, tiles_p, tiles_per_s):
    return jnp.where(i < tiles_p, 0, 1 + (i - tiles_p) // tiles_per_s)


def _mod_table(*vecs):
    r, d = vecs[0].shape
    pad = [jnp.zeros((r, d), F32)] * (V7X_SUBLANES - len(vecs))
    return jnp.stack(list(vecs) + pad, axis=1)


def _ln_mod(u, g, b, shift, scale):
    mu = jnp.mean(u, -1, keepdims=True)
    var = jnp.mean(jnp.square(u - mu), -1, keepdims=True)
    xn = (u - mu) * lax.rsqrt(var + LN_EPS) * g + b
    return xn, xn * (1.0 + scale) + shift


def _proj_res_ln_kernel(a_ref, w_ref, x_ref, mod_ref, ln_ref, wrh_ref, wrl_ref, br_ref, x1_ref, hm_ref, lg_ref):
    y = jnp.dot(a_ref[...], w_ref[...], preferred_element_type=F32)
    u = DEEPNORM_ALPHA * x_ref[...] + mod_ref[0, 0:1, :] * y
    x1, hm = _ln_mod(u, ln_ref[0:1, :], ln_ref[1:2, :], mod_ref[0, 1:2, :], mod_ref[0, 2:3, :])
    x1_ref[...] = x1
    hm_hi = hm.astype(BF16)
    hm_ref[...] = hm_hi
    hm_lo = (hm - hm_hi.astype(F32)).astype(BF16)
    lg_ref[...] = (jnp.dot(hm_hi, wrh_ref[...], preferred_element_type=F32)
                   + jnp.dot(hm_hi, wrl_ref[...], preferred_element_type=F32)
                   + jnp.dot(hm_lo, wrh_ref[...], preferred_element_type=F32) + br_ref[...])


def _proj_res_ln(a, w, x, mod, ln, w_router, b_router, tiles_p, tiles_per_s):
    t, k = a.shape
    d = w.shape[1]
    ne = w_router.shape[1]
    tm = ROW_TILE
    req = lambda i: (_req_of_tile(i, tiles_p, tiles_per_s), 0, 0)
    wr_hi = w_router.astype(BF16)
    wr_lo = (w_router - wr_hi.astype(F32)).astype(BF16)
    return pl.pallas_call(
        _proj_res_ln_kernel,
        grid=(t // tm,),
        in_specs=[pl.BlockSpec((tm, k), lambda i: (i, 0)),
                  pl.BlockSpec((k, d), lambda i: (0, 0), pipeline_mode=pl.Buffered(1)),
                  pl.BlockSpec((tm, d), lambda i: (i, 0)),
                  pl.BlockSpec((1, V7X_SUBLANES, d), req),
                  pl.BlockSpec((2, d), lambda i: (0, 0)),
                  pl.BlockSpec((d, ne), lambda i: (0, 0)),
                  pl.BlockSpec((d, ne), lambda i: (0, 0)),
                  pl.BlockSpec((1, ne), lambda i: (0, 0))],
        out_specs=[pl.BlockSpec((tm, d), lambda i: (i, 0)),
                   pl.BlockSpec((tm, d), lambda i: (i, 0)),
                   pl.BlockSpec((tm, ne), lambda i: (i, 0))],
        out_shape=[jax.ShapeDtypeStruct((t, d), F32), jax.ShapeDtypeStruct((t, d), BF16),
                   jax.ShapeDtypeStruct((t, ne), F32)],
        compiler_params=_params("parallel"),
    )(a, w.astype(BF16), x, mod, ln, wr_hi, wr_lo, b_router.reshape(1, ne))


def _combine_ln_kernel(*refs):
    y_refs = refs[:TOP_K]
    gt_ref, x_ref, mod_ref, ln_ref, x2_ref, h_ref = refs[TOP_K:]
    gt = gt_ref[...]
    ym = gt[:, 0:1] * y_refs[0][...].astype(F32)
    for kk in range(1, TOP_K):
        ym = ym + gt[:, kk:kk + 1] * y_refs[kk][...].astype(F32)
    u = DEEPNORM_ALPHA * x_ref[...] + mod_ref[0, 0:1, :] * ym
    x2, h = _ln_mod(u, ln_ref[0:1, :], ln_ref[1:2, :], mod_ref[0, 1:2, :], mod_ref[0, 2:3, :])
    x2_ref[...] = x2
    h_ref[...] = h.astype(h_ref.dtype)


def _combine_ln(yg, gates, x, mod, ln, tiles_p, tiles_per_s):
    t, d = x.shape
    tm = ROW_TILE
    nt = t // tm
    req = lambda i: (_req_of_tile(i, tiles_p, tiles_per_s), 0, 0)
    y_specs = [pl.BlockSpec((tm, d), functools.partial(lambda i, kk: (kk * nt + i, 0), kk=kk))
               for kk in range(TOP_K)]
    return pl.pallas_call(
        _combine_ln_kernel,
        grid=(nt,),
        in_specs=y_specs + [pl.BlockSpec((tm, TOP_K), lambda i: (i, 0)),
                            pl.BlockSpec((tm, d), lambda i: (i, 0)),
                            pl.BlockSpec((1, V7X_SUBLANES, d), req),
                            pl.BlockSpec((2, d), lambda i: (0, 0))],
        out_specs=[pl.BlockSpec((tm, d), lambda i: (i, 0)),
                   pl.BlockSpec((tm, d), lambda i: (i, 0))],
        out_shape=[jax.ShapeDtypeStruct((t, d), F32), jax.ShapeDtypeStruct((t, d), BF16)],
        compiler_params=_params("parallel"),
    )(*([yg] * TOP_K), gates, x, mod, ln)


def _moe_ffn_kernel(be_ref, nused_ref, x_ref, wgu_ref, bgu_ref, wd_ref, bd_ref, o_ref):
    @pl.when(pl.program_id(0) < nused_ref[0])
    def _():
        x = x_ref[...]
        gu = jnp.dot(x, wgu_ref[0], preferred_element_type=F32) + bgu_ref[0]
        x_glu = jnp.minimum(gu[:, :D_FF], SWIGLU_LIMIT)
        x_lin = jnp.clip(gu[:, D_FF:], -SWIGLU_LIMIT, SWIGLU_LIMIT)
        act = x_glu * jax.nn.sigmoid(SWIGLU_ALPHA * x_glu) * (x_lin + 1.0)
        y = jnp.dot(act.astype(BF16), wd_ref[0], preferred_element_type=F32) + bd_ref[0]
        o_ref[...] = y.astype(o_ref.dtype)

    @pl.when(pl.program_id(0) >= nused_ref[0])
    def _():
        o_ref[...] = jnp.zeros_like(o_ref)


def _moe_ffn(xb, block_e, n_used, w_gu, b_gu, w_down, b_down):
    rows, d = xb.shape
    nb = rows // MOE_TILE_M
    grid_spec = pltpu.PrefetchScalarGridSpec(
        num_scalar_prefetch=2,
        grid=(nb,),
        in_specs=[pl.BlockSpec((MOE_TILE_M, d), lambda i, be, nu: (i, 0)),
                  pl.BlockSpec((1, d, 2 * D_FF), lambda i, be, nu: (be[i], 0, 0)),
                  pl.BlockSpec((1, 1, 2 * D_FF), lambda i, be, nu: (be[i], 0, 0)),
                  pl.BlockSpec((1, D_FF, d), lambda i, be, nu: (be[i], 0, 0)),
                  pl.BlockSpec((1, 1, d), lambda i, be, nu: (be[i], 0, 0))],
        out_specs=pl.BlockSpec((MOE_TILE_M, d), lambda i, be, nu: (i, 0)),
    )
    return pl.pallas_call(
        _moe_ffn_kernel,
        grid_spec=grid_spec,
        out_shape=jax.ShapeDtypeStruct((rows, d), BF16),
        compiler_params=_params("arbitrary"),
    )(block_e, n_used, xb, w_gu, b_gu.reshape(N_EXPERTS, 1, 2 * D_FF), w_down,
      b_down.reshape(N_EXPERTS, 1, d))


def _moe(hm, logits, w_gu, b_gu, w_down, b_down):
    T, d = hm.shape
    tm = MOE_TILE_M
    top_v, top_e = lax.top_k(logits, TOP_K)
    gates = jax.nn.softmax(top_v, -1)
    n = T * TOP_K
    flat_e = top_e.reshape(-1)
    onehot = (flat_e[:, None] == jnp.arange(N_EXPERTS, dtype=jnp.int32)[None, :]).astype(jnp.int32)
    csum = jnp.cumsum(onehot, axis=0)
    counts = csum[-1]
    rank = jnp.sum((csum - 1) * onehot, axis=1)
    padded = (counts + tm - 1) // tm * tm
    pad_end = jnp.cumsum(padded)
    pad_start = pad_end - padded
    start = jnp.cumsum(counts) - counts
    pos = pad_start[flat_e] + rank
    nb = n // tm + N_EXPERTS
    order = jnp.argsort(flat_e)
    n_used = (pad_end[-1] // tm).astype(jnp.int32)
    blk = jnp.arange(nb, dtype=jnp.int32)
    block_e = jnp.minimum(jnp.searchsorted(pad_end, blk * tm, side='right'), N_EXPERTS - 1).astype(jnp.int32)
    last_e = block_e[jnp.maximum(n_used - 1, 0)]
    block_e = jnp.where(blk < n_used, block_e, last_e)
    r = jnp.arange(nb * tm, dtype=jnp.int32)
    row_e = jnp.repeat(block_e, tm)
    off = r - pad_start[row_e]
    valid = (off < counts[row_e]) & (r < pad_end[-1])
    src = jnp.clip(start[row_e] + off, 0, n - 1)
    row_tok = jnp.where(valid, order[src] // TOP_K, 0)
    xb = hm[row_tok]
    yb = _moe_ffn(xb, block_e, n_used.reshape(1), w_gu, b_gu, w_down, b_down)
    pos_kt = jnp.transpose(pos.reshape(T, TOP_K)).reshape(-1)
    return yb[pos_kt], gates


def _softmax_pv(s_list, v_list):
    m = functools.reduce(jnp.maximum, [jnp.max(s, -1, keepdims=True) for s in s_list])
    p_list = [jnp.exp(s - m) for s in s_list]
    denom = functools.reduce(jnp.add, [jnp.sum(p, -1, keepdims=True) for p in p_list])
    o = functools.reduce(jnp.add, [jnp.dot(p.astype(BF16), v, preferred_element_type=F32)
                                   for p, v in zip(p_list, v_list)])
    return o / denom


def _mha_kernel(q_ref, k_ref, v_ref, o_ref, *, heads, dk, dv, scale):
    for h in range(heads):
        q = q_ref[:, h * dk:(h + 1) * dk]
        k = k_ref[:, h * dk:(h + 1) * dk]
        s = lax.dot_general(q, k, _NT, preferred_element_type=F32) * scale
        o = _softmax_pv([s], [v_ref[:, h * dv:(h + 1) * dv]])
        o_ref[:, h * dv:(h + 1) * dv] = o.astype(o_ref.dtype)


def _mha(q, k, v, bsz, lq, lk, heads, dk, dv, scale, q_col=0, k_col=0, v_col=0):
    tq = min(ATTN_TILE_Q, lq)
    nq = lq // tq
    return pl.pallas_call(
        functools.partial(_mha_kernel, heads=heads, dk=dk, dv=dv, scale=scale),
        grid=(bsz, nq),
        in_specs=[pl.BlockSpec((tq, heads * dk), lambda i, j: (i * nq + j, q_col)),
                  pl.BlockSpec((lk, heads * dk), lambda i, j: (i, k_col)),
                  pl.BlockSpec((lk, heads * dv), lambda i, j: (i, v_col))],
        out_specs=pl.BlockSpec((tq, heads * dv), lambda i, j: (i * nq + j, 0)),
        out_shape=jax.ShapeDtypeStruct((bsz * lq, heads * dv), BF16),
        compiler_params=_params("parallel", "arbitrary"),
    )(q, k, v)


def _na_kernel(q_ref, k_ref, v_ref, kc_ref, vc_ref, bias_ref, o_ref, *, heads, d, scale, grid_rows, win):
    rows_per_step = bias_ref.shape[0]
    for rr in range(rows_per_step):
        r = pl.program_id(1) * rows_per_step + rr
        rs = jnp.clip(r - NA_WR // 2, 0, grid_rows - win)
        start = pl.multiple_of(rs * GRID_W, GRID_W)
        for h in range(heads):
            hs = slice(h * d, (h + 1) * d)
            q = q_ref[rr * GRID_W:(rr + 1) * GRID_W, hs]
            k_loc = k_ref[pl.ds(start, win * GRID_W), hs]
            v_loc = v_ref[pl.ds(start, win * GRID_W), hs]
            s_loc = lax.dot_general(q, k_loc, _NT, preferred_element_type=F32) * scale + bias_ref[rr, h]
            s_ctx = lax.dot_general(q, kc_ref[0, :, hs], _NT, preferred_element_type=F32) * scale
            o = _softmax_pv([s_loc, s_ctx], [v_loc, vc_ref[0, :, hs]])
            o_ref[rr * GRID_W:(rr + 1) * GRID_W, hs] = o.astype(o_ref.dtype)


def _na_bias(rpb, grid_rows, win):
    qc = np.arange(GRID_W)
    kc = np.arange(GRID_W)
    qcs = np.clip(qc - NA_WC // 2, 0, GRID_W - NA_WC)
    col_ok = (kc[None, :] >= qcs[:, None]) & (kc[None, :] < qcs[:, None] + NA_WC)
    col_idx = np.clip(kc[None, :] - qc[:, None] + NA_WC - 1, 0, 2 * NA_WC - 2)
    r = np.arange(grid_rows)
    rs = np.clip(r - NA_WR // 2, 0, grid_rows - win)
    row_idx = rs[:, None] + np.arange(win)[None, :] - r[:, None] + NA_WR - 1
    bias = rpb[:, row_idx][:, :, :, col_idx]
    bias = jnp.where(jnp.asarray(col_ok)[None, None, None], bias.astype(F32), -1e30)
    bias = jnp.transpose(bias, (1, 0, 3, 2, 4))
    return bias.reshape(grid_rows, rpb.shape[0], GRID_W, win * GRID_W)


def _na_latent(qkv, row0, bsz, L, k_ctx, v_ctx, rpb):
    heads = rpb.shape[0]
    hd = qkv.shape[1] // 3
    d = hd // heads
    lc = k_ctx.shape[1]
    grid_rows = L // GRID_W
    win = min(NA_WR, grid_rows)
    rps = NA_ROWS_PER_STEP
    tq = rps * GRID_W
    nq = L // tq
    bias = _na_bias(rpb, grid_rows, win)
    return pl.pallas_call(
        functools.partial(_na_kernel, heads=heads, d=d, scale=d ** -0.5, grid_rows=grid_rows, win=win),
        grid=(bsz, nq),
        in_specs=[pl.BlockSpec((tq, hd), lambda i, j: (row0 // tq + i * nq + j, 0)),
                  pl.BlockSpec((L, hd), lambda i, j: (row0 // L + i, 1)),
                  pl.BlockSpec((L, hd), lambda i, j: (row0 // L + i, 2)),
                  pl.BlockSpec((1, lc, hd), lambda i, j: (i, 0, 0)),
                  pl.BlockSpec((1, lc, hd), lambda i, j: (i, 0, 0)),
                  pl.BlockSpec((rps, heads, GRID_W, win * GRID_W), lambda i, j: (j, 0, 0, 0))],
        out_specs=pl.BlockSpec((tq, hd), lambda i, j: (i * nq + j, 0)),
        out_shape=jax.ShapeDtypeStruct((bsz * L, hd), BF16),
        compiler_params=_params("parallel", "arbitrary"),
    )(qkv, qkv, qkv, k_ctx, v_ctx, bias)


def _conv_silu_kernel(x_ref, w_ref, b_ref, o_ref, pad_ref):
    L = x_ref.shape[0]
    zeros = jnp.zeros((CONV_HALO, pad_ref.shape[1]), F32)
    pad_ref[0:CONV_HALO, :] = zeros
    pad_ref[CONV_HALO + L:, :] = zeros
    pad_ref[CONV_HALO:CONV_HALO + L, :] = x_ref[...]
    half = SSM_CONV_W // 2
    acc = b_ref[...] + w_ref[0:1, :] * pad_ref[CONV_HALO - half:CONV_HALO - half + L, :]
    for t in range(1, SSM_CONV_W):
        off = CONV_HALO - half + t
        acc = acc + w_ref[t:t + 1, :] * pad_ref[off:off + L, :]
    o_ref[0] = acc * jax.nn.sigmoid(acc)


def _conv_silu(x, row0, bsz, L, w, b):
    C = x.shape[1]
    tc = CONV_TILE_C
    return pl.pallas_call(
        _conv_silu_kernel,
        grid=(bsz, C // tc),
        in_specs=[pl.BlockSpec((L, tc), lambda i, j: (row0 // L + i, j)),
                  pl.BlockSpec((SSM_CONV_W, tc), lambda i, j: (0, j)),
                  pl.BlockSpec((1, tc), lambda i, j: (0, j))],
        out_specs=pl.BlockSpec((1, L, tc), lambda i, j: (i, 0, j)),
        out_shape=jax.ShapeDtypeStruct((bsz, L, C), F32),
        scratch_shapes=[pltpu.VMEM((L + 2 * CONV_HALO, tc), F32)],
        compiler_params=_params("parallel", "parallel"),
    )(x, w, b.reshape(1, C))


def _gate_norm_kernel(yf_ref, yb_ref, z_ref, nw_ref, o_ref):
    z = z_ref[...]
    y = (yf_ref[...] + yb_ref[...]) * (z * jax.nn.sigmoid(z))
    for g in range(SSM_GROUPS):
        cols = slice(g * SSM_GROUP_W, (g + 1) * SSM_GROUP_W)
        seg = y[:, cols]
        ms = jnp.mean(jnp.square(seg), -1, keepdims=True)
        o_ref[:, cols] = (seg * lax.rsqrt(ms + SSM_NORM_EPS) * nw_ref[:, cols]).astype(o_ref.dtype)


def _gate_norm(y_f, y_b, z, row0, norm_w):
    n, c = y_f.shape
    tm = ROW_TILE
    return pl.pallas_call(
        _gate_norm_kernel,
        grid=(n // tm,),
        in_specs=[pl.BlockSpec((tm, c), lambda i: (i, 0)),
                  pl.BlockSpec((tm, c), lambda i: (i, 0)),
                  pl.BlockSpec((tm, c), lambda i: (row0 // tm + i, 0)),
                  pl.BlockSpec((1, c), lambda i: (0, 0))],
        out_specs=pl.BlockSpec((tm, c), lambda i: (i, 0)),
        out_shape=jax.ShapeDtypeStruct((n, c), BF16),
        compiler_params=_params("parallel"),
    )(y_f, y_b, z, norm_w.reshape(1, c))


def _ssd_kernel(xf_ref, xb_ref, bf_ref, bb_ref, cf_ref, cb_ref, dtf_ref, dtb_ref, dttf_ref, dttb_ref,
                a_ref, at_ref, dsk_ref, h0_ref, yf_ref, yb_ref, hl_ref):
    c = pl.program_id(2)
    Q = SSM_CHUNK
    P = SSM_HEAD_DIM

    @pl.when(c == 0)
    def _():
        hl_ref[...] = h0_ref[...]

    row = lax.broadcasted_iota(jnp.int32, (Q, Q), 0)
    col = lax.broadcasted_iota(jnp.int32, (Q, Q), 1)
    lower = (col <= row)
    upper = (col >= row)
    lower_f = lower.astype(F32)
    upper_f = upper.astype(F32)
    lane = lax.broadcasted_iota(jnp.int32, (Q, V7X_LANES), 1)
    lo_half = lane < P
    hi = lax.Precision.HIGHEST

    dirs = ((0, xf_ref, bf_ref, cf_ref, dtf_ref, dttf_ref, yf_ref, lower, lower_f, upper_f, Q - 1),
            (1, xb_ref, bb_ref, cb_ref, dtb_ref, dttb_ref, yb_ref, upper, upper_f, lower_f, 0))
    for (d, x_ref, b_ref, c_ref, dt_ref, dtt_ref, y_ref, valid, cum_mat, cum_mat_t, last) in dirs:
        dt = dt_ref[0, 0, 0]
        dtt = dtt_ref[0, 0, 0]
        a_row = a_ref[0, d]
        a_col = at_ref[0, d]
        acs = jnp.dot(cum_mat, dt * a_row, preferred_element_type=F32, precision=hi)
        acst = jnp.dot(dtt * a_col, cum_mat_t, preferred_element_type=F32, precision=hi)
        total = acs[last:last + 1, :]
        w_end = jnp.exp(total - acs) * dt
        e_acs = jnp.exp(acs)
        e_tot_t = jnp.exp(acst[:, last:last + 1])
        bmat = b_ref[0].astype(BF16)
        cmat = c_ref[0].astype(BF16)
        cbm = lax.dot_general(cmat, bmat, _NT, preferred_element_type=F32)
        h_prev = hl_ref[0, d]
        y_off = lax.dot_general(cmat, h_prev.astype(BF16), _NT, preferred_element_type=F32)
        xw_slabs = []
        for s in range(SSM_HPG // 2):
            h0, h1 = 2 * s, 2 * s + 1
            x_slab = x_ref[0, :, s * V7X_LANES:(s + 1) * V7X_LANES]
            y_slab = None
            for hh, keep in ((h0, lo_half), (h1, ~lo_half)):
                seg = acs[:, hh:hh + 1] - acst[hh:hh + 1, :]
                decay = jnp.where(valid, jnp.exp(jnp.where(valid, seg, 0.0)), 0.0)
                m = (cbm * decay * dtt[hh:hh + 1, :]).astype(BF16)
                part = jnp.dot(m, jnp.where(keep, x_slab, 0.0).astype(BF16), preferred_element_type=F32)
                y_slab = part if y_slab is None else y_slab + part
            scale_off = jnp.where(lo_half, e_acs[:, h0:h0 + 1], e_acs[:, h1:h1 + 1])
            skip = dsk_ref[0, d, :, s * V7X_LANES:(s + 1) * V7X_LANES]
            y_slab = y_slab + y_off[:, s * V7X_LANES:(s + 1) * V7X_LANES] * scale_off + skip * x_slab
            y_ref[0, :, s * V7X_LANES:(s + 1) * V7X_LANES] = y_slab
            w_slab = jnp.where(lo_half, w_end[:, h0:h0 + 1], w_end[:, h1:h1 + 1])
            xw_slabs.append((x_slab * w_slab).astype(BF16))
        xw = jnp.concatenate(xw_slabs, axis=-1)
        states = lax.dot_general(xw, bmat, (((0,), (0,)), ((), ())), preferred_element_type=F32)
        for hh in range(SSM_HPG):
            rows = slice(hh * P, (hh + 1) * P)
            hl_ref[0, d, rows, :] = h_prev[rows, :] * e_tot_t[hh:hh + 1, :] + states[rows, :]


def _ssd(xbc, dt, a, d_skip, h0):
    bsz, L, _ = xbc.shape
    nc = L // SSM_CHUNK
    G, HPG, Q, N = SSM_GROUPS, SSM_HPG, SSM_CHUNK, SSM_STATE
    gw = SSM_GROUP_W
    dt_g = jnp.transpose(dt.reshape(bsz, L, 2, G, HPG), (2, 0, 3, 1, 4))
    dtt_g = jnp.transpose(dt_g, (0, 1, 2, 4, 3))
    a_g = jnp.transpose(a.reshape(2, G, 1, HPG), (1, 0, 2, 3))
    at_g = jnp.transpose(a_g, (0, 1, 3, 2))
    dsk = jnp.repeat(d_skip.reshape(2, G, HPG), SSM_HEAD_DIM, axis=-1)
    dsk = jnp.transpose(dsk, (1, 0, 2)).reshape(G, 2, 1, gw)
    nb_x = SSM_D_INNER // N
    fwd = lambda c: c
    bwd = lambda c: nc - 1 - c
    x_spec = lambda f: pl.BlockSpec((1, Q, gw), lambda b, g, c: (b, f(c), g))
    b_spec = lambda f: pl.BlockSpec((1, Q, N), lambda b, g, c: (b, f(c), nb_x + g))
    c_spec = lambda f: pl.BlockSpec((1, Q, N), lambda b, g, c: (b, f(c), nb_x + G + g))
    dt_spec = lambda d, f: pl.BlockSpec((1, 1, 1, Q, HPG), lambda b, g, c: (d, b, g, f(c), 0))
    dtt_spec = lambda d, f: pl.BlockSpec((1, 1, 1, HPG, Q), lambda b, g, c: (d, b, g, 0, f(c)))
    y_spec = lambda f: pl.BlockSpec((1, Q, gw), lambda b, g, c: (b, f(c), g))
    h_spec = pl.BlockSpec((1, 2, gw, N), lambda b, g, c: (b, 0, g, 0))
    return pl.pallas_call(
        _ssd_kernel,
        grid=(bsz, G, nc),
        in_specs=[x_spec(fwd), x_spec(bwd), b_spec(fwd), b_spec(bwd), c_spec(fwd), c_spec(bwd),
                  dt_spec(0, fwd), dt_spec(1, bwd), dtt_spec(0, fwd), dtt_spec(1, bwd),
                  pl.BlockSpec((1, 2, 1, HPG), lambda b, g, c: (g, 0, 0, 0)),
                  pl.BlockSpec((1, 2, HPG, 1), lambda b, g, c: (g, 0, 0, 0)),
                  pl.BlockSpec((1, 2, 1, gw), lambda b, g, c: (g, 0, 0, 0)),
                  h_spec],
        out_specs=[y_spec(fwd), y_spec(bwd), h_spec],
        out_shape=[jax.ShapeDtypeStruct((bsz, L, SSM_D_INNER), F32),
                   jax.ShapeDtypeStruct((bsz, L, SSM_D_INNER), F32),
                   jax.ShapeDtypeStruct(h0.shape, F32)],
        compiler_params=_params("parallel", "parallel", "arbitrary"),
    )(xbc, xbc, xbc, xbc, xbc, xbc, dt_g, dt_g, dtt_g, dtt_g, a_g, at_g, dsk, h0)


def _layer_norm(x, g, b, eps=1e-5):
    mu = jnp.mean(x, -1, keepdims=True)
    var = jnp.mean(jnp.square(x - mu), -1, keepdims=True)
    return ((x - mu) * lax.rsqrt(var + eps)) * g + b


def _rms_norm(x, g, eps=1e-6):
    return (x * lax.rsqrt(jnp.mean(jnp.square(x), -1, keepdims=True) + eps)) * g


def _rope_2d(x):
    L = x.shape[1]
    half = x.shape[-1] // 2
    quarter = half // 2
    t = jnp.arange(L)
    inv = ROPE_THETA ** (-jnp.arange(quarter, dtype=F32) / quarter)
    ang_r = (t // GRID_W).astype(F32)[:, None] * inv
    ang_c = (t % GRID_W).astype(F32)[:, None] * inv
    bshape = (L,) + (1,) * (x.ndim - 3) + (quarter,)

    def rot(v, ang):
        cos = jnp.cos(ang).reshape(bshape).astype(v.dtype)
        sin = jnp.sin(ang).reshape(bshape).astype(v.dtype)
        v1, v2 = v[..., :quarter], v[..., quarter:]
        return jnp.concatenate([v1 * cos - v2 * sin, v2 * cos + v1 * sin], -1)

    return jnp.concatenate([rot(x[..., :half], ang_r), rot(x[..., half:], ang_c)], -1)


def _mla_pack_q(q_nope, q_pe):
    b, L, h, _ = q_nope.shape
    pad = jnp.zeros((b, L, h, MLA_QK_PAD - MLA_NOPE - MLA_ROPE_DIM), BF16)
    return jnp.concatenate([q_nope.astype(BF16), q_pe.astype(BF16), pad], -1).reshape(b, L, h * MLA_QK_PAD)


def _mla_pack_k(k_nope, k_pe):
    b, L, h, _ = k_nope.shape
    kpe = jnp.broadcast_to(k_pe.astype(BF16)[:, :, None, :], (b, L, h, MLA_ROPE_DIM))
    pad = jnp.zeros((b, L, h, MLA_QK_PAD - MLA_NOPE - MLA_ROPE_DIM), BF16)
    return jnp.concatenate([k_nope.astype(BF16), kpe, pad], -1).reshape(b, L, h * MLA_QK_PAD)


def _ssm_mixer(h, bp, lp, bs, ls, h0_s, w_in, conv_w, conv_b, dt_bias, a_log, d_skip, norm_w):
    tp = bp * lp
    w_in = w_in.astype(BF16)
    z = _matmul(h, w_in[:, :SSM_D_INNER])
    xbc = _matmul(h, w_in[:, SSM_D_INNER:SSM_D_INNER + SSM_CONV_CH])
    dt_raw = _matmul(h, w_in[:, SSM_D_INNER + SSM_CONV_CH:])
    dt = jax.nn.softplus(dt_raw.reshape(-1, 2, SSM_HEADS) + dt_bias)
    a = -jnp.exp(a_log)
    acts = []
    st_p = None
    for (t0, bsz, L, h0) in ((0, bp, lp, None), (tp, bs, ls, h0_s)):
        n_tok = bsz * L
        xa = _conv_silu(xbc, t0, bsz, L, conv_w, conv_b)
        if h0 is None:
            h0 = jnp.zeros((bsz, 2, SSM_D_INNER, SSM_STATE), F32)
        else:
            h0 = h0.reshape(bsz, 2, SSM_D_INNER, SSM_STATE)
        y_f, y_b, h_last = _ssd(xa, dt[t0:t0 + n_tok].reshape(bsz, L, 2, SSM_HEADS), a, d_skip, h0)
        if st_p is None:
            st_p = h_last.reshape(bsz, 2, SSM_HEADS, SSM_HEAD_DIM, SSM_STATE)
        acts.append(_gate_norm(y_f.reshape(n_tok, SSM_D_INNER), y_b.reshape(n_tok, SSM_D_INNER), z, t0, norm_w))
    return jnp.concatenate(acts, 0), st_p


def _attn_mixer(h, bp, lp, bs, ls, c_k, c_v, c_ckv, c_kpe, w_in, rpb, q_norm, w_uq, kv_norm, w_ukv):
    tp, ts = bp * lp, bs * ls
    w_in = w_in.astype(BF16)
    o_qkv, o_cq, o_ckv = 3 * NA_QKV, 3 * NA_QKV + MLA_Q_LORA, 3 * NA_QKV + MLA_Q_LORA + MLA_KV_LORA
    qkv = _matmul(h, w_in[:, :o_qkv], out_dtype=BF16)
    kv_p32 = _matmul(h[:tp], w_in[:, NA_QKV:o_qkv])
    n_lat = w_in.shape[1] - o_qkv
    lat = _matmul(h, jnp.pad(w_in[:, o_qkv:], ((0, 0), (0, MLA_LATENT_PAD - n_lat))))
    cq, ckv, kpe = lat[:, :MLA_Q_LORA], lat[:, MLA_Q_LORA:o_ckv - o_qkv], lat[:, o_ckv - o_qkv:n_lat]
    ckv = _rms_norm(ckv, kv_norm, MLA_NORM_EPS)
    q_all = _matmul(_rms_norm(cq, q_norm, MLA_NORM_EPS).astype(BF16), w_uq)
    lc = c_ckv.shape[1]
    kv_all = _matmul(jnp.concatenate([ckv, c_ckv.reshape(-1, MLA_KV_LORA)], 0).astype(BF16), w_ukv)

    hq = MLA_NOPE + MLA_ROPE_DIM
    hkv = MLA_NOPE + MLA_V
    mla_scale = hq ** -0.5
    mla = functools.partial(_mha, heads=MLA_HEADS, dk=MLA_QK_PAD, dv=MLA_V, scale=mla_scale)

    o_na_p = _mha(qkv, qkv, qkv, bp, lp, lp, NA_HEADS, NA_HEAD_DIM, NA_HEAD_DIM, NA_HEAD_DIM ** -0.5,
                  q_col=0, k_col=1, v_col=2)
    q_p = q_all[:tp].reshape(bp, lp, MLA_HEADS, hq)
    kv_p = kv_all[:tp].reshape(bp, lp, MLA_HEADS, hkv)
    o_mla_p = mla(_mla_pack_q(q_p[..., :MLA_NOPE], q_p[..., MLA_NOPE:]).reshape(tp, -1),
                  _mla_pack_k(kv_p[..., :MLA_NOPE], kpe[:tp].reshape(bp, lp, -1)).reshape(tp, -1),
                  kv_p[..., MLA_NOPE:].astype(BF16).reshape(tp, -1), bsz=bp, lq=lp, lk=lp)

    o_na_s = _na_latent(qkv, tp, bs, ls, c_k.astype(BF16).reshape(bs, -1, NA_QKV),
                        c_v.astype(BF16).reshape(bs, -1, NA_QKV), rpb)
    q_s = q_all[tp:].reshape(bs, ls, MLA_HEADS, hq)
    kv_s = jnp.concatenate([kv_all[tp:tp + ts].reshape(bs, ls, MLA_HEADS, hkv),
                            kv_all[tp + ts:].reshape(bs, lc, MLA_HEADS, hkv)], 1)
    kpe_s = jnp.concatenate([_rope_2d(kpe[tp:].reshape(bs, ls, -1)), c_kpe], 1)
    o_mla_s = mla(_mla_pack_q(q_s[..., :MLA_NOPE], _rope_2d(q_s[..., MLA_NOPE:])).reshape(ts, -1),
                  _mla_pack_k(kv_s[..., :MLA_NOPE], kpe_s).reshape(bs * (ls + lc), -1),
                  kv_s[..., MLA_NOPE:].astype(BF16).reshape(bs * (ls + lc), -1), bsz=bs, lq=ls, lk=ls + lc)

    o_cat = jnp.concatenate([jnp.concatenate([o_na_p, o_na_s], 0), jnp.concatenate([o_mla_p, o_mla_s], 0)], -1)
    new = (kv_p32[:, :NA_QKV].reshape(bp, lp, NA_HEADS, NA_HEAD_DIM),
           kv_p32[:, NA_QKV:].reshape(bp, lp, NA_HEADS, NA_HEAD_DIM),
           ckv[:tp].reshape(bp, lp, -1), kpe[:tp].reshape(bp, lp, -1))
    return o_cat, new


def _modulation(cvec, w, b):
    return jnp.split(jax.nn.silu(cvec) @ w + b, 6, axis=-1)


def kernel(x_prompt, x_sample, c, cache_na_k, cache_na_v, cache_mla_ckv, cache_mla_kpe, state_ssm,
           c_ctx, w_ada, b_ada, ln_g, ln_b, attn_w_in, na_rpb, mla_q_norm, mla_w_uq, mla_kv_norm,
           mla_w_ukv, attn_w_out, ssm_w_in, ssm_conv_w, ssm_conv_b, ssm_dt_bias, ssm_a_log, ssm_d,
           ssm_norm, ssm_w_out, moe_w_router, moe_b_router, moe_w_gu, moe_b_gu, moe_w_down, moe_b_down):
    bp, lp, D = x_prompt.shape
    bs, ls, _ = x_sample.shape
    tp = bp * lp
    tiles_p, tiles_per_s = tp // ROW_TILE, ls // ROW_TILE
    new_k, new_v, new_ckv, new_kpe, new_ssm = [], [], [], [], []

    cond = jnp.concatenate([c_ctx[None, :], c], 0)
    mods = [_modulation(cond, w_ada[l], b_ada[l]) for l in range(DEPTH)]
    zero = jnp.zeros_like(mods[0][0])

    x = jnp.concatenate([x_prompt.reshape(tp, D), x_sample.reshape(bs * ls, D)], 0)
    shift1, scale1 = mods[0][0], mods[0][1]
    h = jnp.concatenate(
        [(x_prompt * (1.0 + scale1[0]) + shift1[0]).reshape(tp, D),
         (x_sample * (1.0 + scale1[1:, None, :]) + shift1[1:, None, :]).reshape(bs * ls, D)], 0).astype(BF16)

    for l in range(DEPTH):
        m = mods[l]
        if l % 2 == 0:
            i = l // 2
            act, (k_c, v_c, ckv_c, kpe_c) = _attn_mixer(
                h, bp, lp, bs, ls, cache_na_k[:, i], cache_na_v[:, i], cache_mla_ckv[:, i], cache_mla_kpe[:, i],
                attn_w_in[i], na_rpb[i], mla_q_norm[i], mla_w_uq[i], mla_kv_norm[i], mla_w_ukv[i])
            w_out = attn_w_out[i]
            new_k.append(k_c)
            new_v.append(v_c)
            new_ckv.append(ckv_c)
            new_kpe.append(kpe_c)
        else:
            j = l // 2
            act, st = _ssm_mixer(h, bp, lp, bs, ls, state_ssm[:, j], ssm_w_in[j], ssm_conv_w[j], ssm_conv_b[j],
                                 ssm_dt_bias[j], ssm_a_log[j], ssm_d[j], ssm_norm[j])
            w_out = ssm_w_out[j]
            new_ssm.append(st)
        x1, hm, logits = _proj_res_ln(act, w_out, x, _mod_table(m[2], m[3], m[4]),
                                      jnp.stack([ln_g[l, 0], ln_b[l, 0]]), moe_w_router[l], moe_b_router[l],
                                      tiles_p, tiles_per_s)
        yg, gates = _moe(hm, logits, _cast_bf16(moe_w_gu, l), moe_b_gu[l], _cast_bf16(moe_w_down, l),
                         moe_b_down[l])
        nxt = mods[l + 1] if l + 1 < DEPTH else (zero, zero)
        x, h = _combine_ln(yg, gates, x1, _mod_table(m[5], nxt[0], nxt[1]),
                           jnp.stack([ln_g[l, 1], ln_b[l, 1]]), tiles_p, tiles_per_s)
    return (x[:tp].reshape(bp, lp, D), x[tp:].reshape(bs, ls, D), jnp.stack(new_k, 1), jnp.stack(new_v, 1),
            jnp.stack(new_ckv, 1), jnp.stack(new_kpe, 1), jnp.stack(new_ssm, 1))
```

```python
import functools
import math

import jax
import jax.numpy as jnp
import numpy as np
from jax import lax
from jax.experimental import pallas as pl
from jax.experimental.pallas import tpu as pltpu

D_MODEL = 2048
DEPTH = 4
GRID_W = 64

NA_HEADS = 8
NA_HEAD_DIM = 128
NA_WR = 8
NA_WC = 16

MLA_HEADS = 8
MLA_Q_LORA = 768
MLA_KV_LORA = 512
MLA_NOPE = 128
MLA_ROPE_DIM = 64
MLA_V = 128
MLA_QK_PAD = 256
MLA_LATENT_PAD = 1536
ROPE_THETA = 10000.0

NA_QKV = NA_HEADS * NA_HEAD_DIM

SSM_D_INNER = 2 * D_MODEL
SSM_HEAD_DIM = 64
SSM_HEADS = SSM_D_INNER // SSM_HEAD_DIM
SSM_GROUPS = 8
SSM_HPG = SSM_HEADS // SSM_GROUPS
SSM_GROUP_W = SSM_HPG * SSM_HEAD_DIM
SSM_STATE = 128
SSM_SPLIT_W = 6 * SSM_HPG
SSM_CONV_W = 5
SSM_CHUNK = 128
SSM_CONV_CH = SSM_D_INNER + 2 * SSM_GROUPS * SSM_STATE

N_EXPERTS = 32
TOP_K = 4
D_FF = D_MODEL // 2
SWIGLU_LIMIT = 7.0
SWIGLU_ALPHA = 1.702

DEEPNORM_ALPHA = (2.0 * DEPTH) ** 0.25
LN_EPS = 1e-5
MASK_NEG = -1e30
SSM_NORM_EPS = 1e-5
MLA_NORM_EPS = 1e-6

V7X_VMEM_LIMIT_BYTES = 48 * 1024 * 1024
V7X_LANES = 128
V7X_SUBLANES = 8
ROW_TILE = 256
MM_TILE_M = 1024
MM_TILE_N = 1024
MOE_TILE_M = 512
CAST_TILE_K = 512
ATTN_TILE_Q = 256
NA_ROWS_PER_STEP = 2
CONV_TILE_ELEMS = 1024 * 1024
CONV_HALO = 8

BF16 = jnp.bfloat16
F32 = jnp.float32
_NT = (((1,), (1,)), ((), ()))


def _params(*sem):
    return pltpu.CompilerParams(dimension_semantics=sem, vmem_limit_bytes=V7X_VMEM_LIMIT_BYTES)


def _mm_kernel(x_ref, w_ref, o_ref):
    o_ref[...] = jnp.dot(x_ref[...], w_ref[...], preferred_element_type=F32).astype(o_ref.dtype)


def _pick_tile(n, pref):
    for t in (pref, 768, 512, 384, 256, 128):
        if t <= pref and n % t == 0:
            return t
    return n


def _matmul(x, w, out_dtype=F32):
    m, k = x.shape
    n = w.shape[1]
    tm = _pick_tile(m, MM_TILE_M)
    tn = _pick_tile(n, MM_TILE_N)
    return pl.pallas_call(
        _mm_kernel,
        grid=(m // tm, n // tn),
        in_specs=[pl.BlockSpec((tm, k), lambda i, j: (i, 0)),
                  pl.BlockSpec((k, tn), lambda i, j: (0, j))],
        out_specs=pl.BlockSpec((tm, tn), lambda i, j: (i, j)),
        out_shape=jax.ShapeDtypeStruct((m, n), out_dtype),
        compiler_params=_params("parallel", "parallel"),
    )(x.astype(BF16), w.astype(BF16))


def _cast_kernel(x_ref, o_ref):
    o_ref[...] = x_ref[0].astype(o_ref.dtype)


def _cast_bf16(w, l):
    _, e, k, n = w.shape
    tk = _pick_tile(k, CAST_TILE_K)
    return pl.pallas_call(
        _cast_kernel,
        grid=(e, k // tk),
        in_specs=[pl.BlockSpec((1, 1, tk, n), lambda i, j: (l, i, j, 0))],
        out_specs=pl.BlockSpec((1, tk, n), lambda i, j: (i, j, 0)),
        out_shape=jax.ShapeDtypeStruct((e, k, n), BF16),
        compiler_params=_params("parallel", "parallel"),
    )(w)


def _req_of_tile(i, tiles_p, tiles_per_s):
    return jnp.where(i < tiles_p, 0, 1 + (i - tiles_p) // tiles_per_s)


def _mod_table(*vecs):
    r, d = vecs[0].shape
    pad = [jnp.zeros((r, d), F32)] * (V7X_SUBLANES - len(vecs))
    return jnp.stack(list(vecs) + pad, axis=1)


def _ln_mod(u, g, b, shift, scale):
    mu = jnp.mean(u, -1, keepdims=True)
    var = jnp.mean(jnp.square(u - mu), -1, keepdims=True)
    xn = (u - mu) * lax.rsqrt(var + LN_EPS) * g + b
    return xn, xn * (1.0 + scale) + shift


def _proj_res_ln_kernel(a_ref, w_ref, x_ref, mod_ref, ln_ref, wrh_ref, wrl_ref, br_ref, x1_ref, hm_ref, lg_ref):
    y = jnp.dot(a_ref[...], w_ref[...], preferred_element_type=F32)
    u = DEEPNORM_ALPHA * x_ref[...] + mod_ref[0, 0:1, :] * y
    x1, hm = _ln_mod(u, ln_ref[0:1, :], ln_ref[1:2, :], mod_ref[0, 1:2, :], mod_ref[0, 2:3, :])
    x1_ref[...] = x1
    hm_hi = hm.astype(BF16)
    hm_ref[...] = hm_hi
    hm_lo = (hm - hm_hi.astype(F32)).astype(BF16)
    lg_ref[...] = (jnp.dot(hm_hi, wrh_ref[...], preferred_element_type=F32)
                   + jnp.dot(hm_hi, wrl_ref[...], preferred_element_type=F32)
                   + jnp.dot(hm_lo, wrh_ref[...], preferred_element_type=F32) + br_ref[...])


def _proj_res_ln(a, w, x, mod, ln, w_router, b_router, tiles_p, tiles_per_s):
    t, k = a.shape
    d = w.shape[1]
    ne = w_router.shape[1]
    tm = ROW_TILE
    req = lambda i: (_req_of_tile(i, tiles_p, tiles_per_s), 0, 0)
    wr_hi = w_router.astype(BF16)
    wr_lo = (w_router - wr_hi.astype(F32)).astype(BF16)
    return pl.pallas_call(
        _proj_res_ln_kernel,
        grid=(t // tm,),
        in_specs=[pl.BlockSpec((tm, k), lambda i: (i, 0)),
                  pl.BlockSpec((k, d), lambda i: (0, 0), pipeline_mode=pl.Buffered(1)),
                  pl.BlockSpec((tm, d), lambda i: (i, 0)),
                  pl.BlockSpec((1, V7X_SUBLANES, d), req),
                  pl.BlockSpec((2, d), lambda i: (0, 0)),
                  pl.BlockSpec((d, ne), lambda i: (0, 0)),
                  pl.BlockSpec((d, ne), lambda i: (0, 0)),
                  pl.BlockSpec((1, ne), lambda i: (0, 0))],
        out_specs=[pl.BlockSpec((tm, d), lambda i: (i, 0)),
                   pl.BlockSpec((tm, d), lambda i: (i, 0)),
                   pl.BlockSpec((tm, ne), lambda i: (i, 0))],
        out_shape=[jax.ShapeDtypeStruct((t, d), F32), jax.ShapeDtypeStruct((t, d), BF16),
                   jax.ShapeDtypeStruct((t, ne), F32)],
        compiler_params=_params("parallel"),
    )(a, w.astype(BF16), x, mod, ln, wr_hi, wr_lo, b_router.reshape(1, ne))


def _combine_ln_kernel(*refs):
    y_refs = refs[:TOP_K]
    gt_ref, x_ref, mod_ref, ln_ref, x2_ref, h_ref = refs[TOP_K:]
    gt = gt_ref[...]
    ym = gt[:, 0:1] * y_refs[0][...].astype(F32)
    for kk in range(1, TOP_K):
        ym = ym + gt[:, kk:kk + 1] * y_refs[kk][...].astype(F32)
    u = DEEPNORM_ALPHA * x_ref[...] + mod_ref[0, 0:1, :] * ym
    x2, h = _ln_mod(u, ln_ref[0:1, :], ln_ref[1:2, :], mod_ref[0, 1:2, :], mod_ref[0, 2:3, :])
    x2_ref[...] = x2
    h_ref[...] = h.astype(h_ref.dtype)


def _combine_ln(yg, gates, x, mod, ln, tiles_p, tiles_per_s):
    t, d = x.shape
    tm = ROW_TILE
    nt = t // tm
    req = lambda i: (_req_of_tile(i, tiles_p, tiles_per_s), 0, 0)
    y_specs = [pl.BlockSpec((tm, d), functools.partial(lambda i, kk: (kk * nt + i, 0), kk=kk))
               for kk in range(TOP_K)]
    return pl.pallas_call(
        _combine_ln_kernel,
        grid=(nt,),
        in_specs=y_specs + [pl.BlockSpec((tm, TOP_K), lambda i: (i, 0)),
                            pl.BlockSpec((tm, d), lambda i: (i, 0)),
                            pl.BlockSpec((1, V7X_SUBLANES, d), req),
                            pl.BlockSpec((2, d), lambda i: (0, 0))],
        out_specs=[pl.BlockSpec((tm, d), lambda i: (i, 0)),
                   pl.BlockSpec((tm, d), lambda i: (i, 0))],
        out_shape=[jax.ShapeDtypeStruct((t, d), F32), jax.ShapeDtypeStruct((t, d), BF16)],
        compiler_params=_params("parallel"),
    )(*([yg] * TOP_K), gates, x, mod, ln)


def _moe_ffn_kernel(be_ref, nused_ref, x_ref, wgu_ref, bgu_ref, wd_ref, bd_ref, o_ref):
    @pl.when(pl.program_id(0) < nused_ref[0])
    def _():
        x = x_ref[...]
        gu = jnp.dot(x, wgu_ref[0], preferred_element_type=F32) + bgu_ref[0]
        x_glu = jnp.minimum(gu[:, :D_FF], SWIGLU_LIMIT)
        x_lin = jnp.clip(gu[:, D_FF:], -SWIGLU_LIMIT, SWIGLU_LIMIT)
        act = x_glu * jax.nn.sigmoid(SWIGLU_ALPHA * x_glu) * (x_lin + 1.0)
        y = jnp.dot(act.astype(BF16), wd_ref[0], preferred_element_type=F32) + bd_ref[0]
        o_ref[...] = y.astype(o_ref.dtype)

    @pl.when(pl.program_id(0) >= nused_ref[0])
    def _():
        o_ref[...] = jnp.zeros_like(o_ref)


def _moe_ffn(xb, block_e, n_used, w_gu, b_gu, w_down, b_down):
    rows, d = xb.shape
    nb = rows // MOE_TILE_M
    grid_spec = pltpu.PrefetchScalarGridSpec(
        num_scalar_prefetch=2,
        grid=(nb,),
        in_specs=[pl.BlockSpec((MOE_TILE_M, d), lambda i, be, nu: (i, 0)),
                  pl.BlockSpec((1, d, 2 * D_FF), lambda i, be, nu: (be[i], 0, 0)),
                  pl.BlockSpec((1, 1, 2 * D_FF), lambda i, be, nu: (be[i], 0, 0)),
                  pl.BlockSpec((1, D_FF, d), lambda i, be, nu: (be[i], 0, 0)),
                  pl.BlockSpec((1, 1, d), lambda i, be, nu: (be[i], 0, 0))],
        out_specs=pl.BlockSpec((MOE_TILE_M, d), lambda i, be, nu: (i, 0)),
    )
    return pl.pallas_call(
        _moe_ffn_kernel,
        grid_spec=grid_spec,
        out_shape=jax.ShapeDtypeStruct((rows, d), BF16),
        compiler_params=_params("arbitrary"),
    )(block_e, n_used, xb, w_gu, b_gu.reshape(N_EXPERTS, 1, 2 * D_FF), w_down,
      b_down.reshape(N_EXPERTS, 1, d))


def _moe(hm, logits, w_gu, b_gu, w_down, b_down):
    T, d = hm.shape
    tm = MOE_TILE_M
    top_v, top_e = lax.top_k(logits, TOP_K)
    gates = jax.nn.softmax(top_v, -1)
    n = T * TOP_K
    flat_e = top_e.reshape(-1)
    iota_n = jnp.arange(n, dtype=jnp.int32)
    sorted_e, order = lax.sort((flat_e, iota_n), num_keys=1, is_stable=True)
    bounds = jnp.searchsorted(sorted_e, jnp.arange(N_EXPERTS + 1, dtype=jnp.int32), side='left').astype(jnp.int32)
    start, counts = bounds[:-1], bounds[1:] - bounds[:-1]
    padded = (counts + tm - 1) // tm * tm
    pad_end = jnp.cumsum(padded)
    pad_start = pad_end - padded
    pos_sorted = pad_start[sorted_e] + iota_n - start[sorted_e]
    _, pos = lax.sort((order, pos_sorted), num_keys=1)
    nb = n // tm + N_EXPERTS
    n_used = (pad_end[-1] // tm).astype(jnp.int32)
    blk = jnp.arange(nb, dtype=jnp.int32)
    block_e = jnp.minimum(jnp.searchsorted(pad_end, blk * tm, side='right'), N_EXPERTS - 1).astype(jnp.int32)
    last_e = block_e[jnp.maximum(n_used - 1, 0)]
    block_e = jnp.where(blk < n_used, block_e, last_e)
    r = jnp.arange(nb * tm, dtype=jnp.int32)
    row_e = jnp.repeat(block_e, tm)
    off = r - pad_start[row_e]
    valid = (off < counts[row_e]) & (r < pad_end[-1])
    src = jnp.clip(start[row_e] + off, 0, n - 1)
    row_tok = jnp.where(valid, order[src] // TOP_K, 0)
    xb = hm[row_tok]
    yb = _moe_ffn(xb, block_e, n_used.reshape(1), w_gu, b_gu, w_down, b_down)
    pos_kt = jnp.transpose(pos.reshape(T, TOP_K)).reshape(-1)
    return yb[pos_kt], gates


def _softmax_pv(s_list, v_list):
    m = functools.reduce(jnp.maximum, [jnp.max(s, -1, keepdims=True) for s in s_list])
    p_list = [jnp.exp(s - m) for s in s_list]
    denom = functools.reduce(jnp.add, [jnp.sum(p, -1, keepdims=True) for p in p_list])
    o = functools.reduce(jnp.add, [jnp.dot(p.astype(BF16), v, preferred_element_type=F32)
                                   for p, v in zip(p_list, v_list)])
    return o / denom


def _mha_kernel(q_ref, k_ref, v_ref, o_ref, *, heads, dk, dv, scale):
    for h in range(heads):
        q = q_ref[:, h * dk:(h + 1) * dk]
        k = k_ref[:, h * dk:(h + 1) * dk]
        s = lax.dot_general(q, k, _NT, preferred_element_type=F32) * scale
        o = _softmax_pv([s], [v_ref[:, h * dv:(h + 1) * dv]])
        o_ref[:, h * dv:(h + 1) * dv] = o.astype(o_ref.dtype)


def _mha(q, k, v, bsz, lq, lk, heads, dk, dv, scale, q_col=0, k_col=0, v_col=0):
    tq = min(ATTN_TILE_Q, lq)
    nq = lq // tq
    return pl.pallas_call(
        functools.partial(_mha_kernel, heads=heads, dk=dk, dv=dv, scale=scale),
        grid=(bsz, nq),
        in_specs=[pl.BlockSpec((tq, heads * dk), lambda i, j: (i * nq + j, q_col)),
                  pl.BlockSpec((lk, heads * dk), lambda i, j: (i, k_col)),
                  pl.BlockSpec((lk, heads * dv), lambda i, j: (i, v_col))],
        out_specs=pl.BlockSpec((tq, heads * dv), lambda i, j: (i * nq + j, 0)),
        out_shape=jax.ShapeDtypeStruct((bsz * lq, heads * dv), BF16),
        compiler_params=_params("parallel", "arbitrary"),
    )(q, k, v)


def _na_kernel(q_ref, k_ref, v_ref, kc_ref, vc_ref, bias_ref, o_ref, *, heads, d, scale, grid_rows, win):
    rows_per_step = bias_ref.shape[0]
    for rr in range(rows_per_step):
        r = pl.program_id(1) * rows_per_step + rr
        rs = jnp.clip(r - NA_WR // 2, 0, grid_rows - win)
        start = pl.multiple_of(rs * GRID_W, GRID_W)
        for h in range(heads):
            hs = slice(h * d, (h + 1) * d)
            q = q_ref[rr * GRID_W:(rr + 1) * GRID_W, hs]
            k_loc = k_ref[pl.ds(start, win * GRID_W), hs]
            v_loc = v_ref[pl.ds(start, win * GRID_W), hs]
            s_loc = lax.dot_general(q, k_loc, _NT, preferred_element_type=F32) * scale + bias_ref[rr, h]
            s_ctx = lax.dot_general(q, kc_ref[0, :, hs], _NT, preferred_element_type=F32) * scale
            o = _softmax_pv([s_loc, s_ctx], [v_loc, vc_ref[0, :, hs]])
            o_ref[rr * GRID_W:(rr + 1) * GRID_W, hs] = o.astype(o_ref.dtype)


def _na_bias(rpb, grid_rows, win):
    qc = np.arange(GRID_W)
    kc = np.arange(GRID_W)
    qcs = np.clip(qc - NA_WC // 2, 0, GRID_W - NA_WC)
    col_ok = (kc[None, :] >= qcs[:, None]) & (kc[None, :] < qcs[:, None] + NA_WC)
    col_idx = np.clip(kc[None, :] - qc[:, None] + NA_WC - 1, 0, 2 * NA_WC - 2)
    r = np.arange(grid_rows)
    rs = np.clip(r - NA_WR // 2, 0, grid_rows - win)
    row_idx = rs[:, None] + np.arange(win)[None, :] - r[:, None] + NA_WR - 1
    bias = rpb[:, row_idx][:, :, :, col_idx]
    bias = jnp.where(jnp.asarray(col_ok)[None, None, None], bias.astype(F32), MASK_NEG)
    bias = jnp.transpose(bias, (1, 0, 3, 2, 4))
    return bias.reshape(grid_rows, rpb.shape[0], GRID_W, win * GRID_W)


def _na_latent(qkv, row0, bsz, L, k_ctx, v_ctx, rpb):
    heads = rpb.shape[0]
    hd = qkv.shape[1] // 3
    d = hd // heads
    lc = k_ctx.shape[1]
    grid_rows = L // GRID_W
    win = min(NA_WR, grid_rows)
    rps = NA_ROWS_PER_STEP
    tq = rps * GRID_W
    nq = L // tq
    bias = _na_bias(rpb, grid_rows, win)
    return pl.pallas_call(
        functools.partial(_na_kernel, heads=heads, d=d, scale=d ** -0.5, grid_rows=grid_rows, win=win),
        grid=(bsz, nq),
        in_specs=[pl.BlockSpec((tq, hd), lambda i, j: (row0 // tq + i * nq + j, 0)),
                  pl.BlockSpec((L, hd), lambda i, j: (row0 // L + i, 1)),
                  pl.BlockSpec((L, hd), lambda i, j: (row0 // L + i, 2)),
                  pl.BlockSpec((1, lc, hd), lambda i, j: (i, 0, 0)),
                  pl.BlockSpec((1, lc, hd), lambda i, j: (i, 0, 0)),
                  pl.BlockSpec((rps, heads, GRID_W, win * GRID_W), lambda i, j: (j, 0, 0, 0))],
        out_specs=pl.BlockSpec((tq, hd), lambda i, j: (i * nq + j, 0)),
        out_shape=jax.ShapeDtypeStruct((bsz * L, hd), BF16),
        compiler_params=_params("parallel", "arbitrary"),
    )(qkv, qkv, qkv, k_ctx, v_ctx, bias)


def _conv_silu_kernel(x_ref, w_ref, b_ref, o_ref, pad_ref):
    L = x_ref.shape[0]
    zeros = jnp.zeros((CONV_HALO, pad_ref.shape[1]), F32)
    pad_ref[0:CONV_HALO, :] = zeros
    pad_ref[CONV_HALO + L:, :] = zeros
    pad_ref[CONV_HALO:CONV_HALO + L, :] = x_ref[...]
    half = SSM_CONV_W // 2
    acc = b_ref[...] + w_ref[0:1, :] * pad_ref[CONV_HALO - half:CONV_HALO - half + L, :]
    for t in range(1, SSM_CONV_W):
        off = CONV_HALO - half + t
        acc = acc + w_ref[t:t + 1, :] * pad_ref[off:off + L, :]
    o_ref[0] = acc * jax.nn.sigmoid(acc)


def _conv_silu(x, row0, bsz, L, w, b):
    C = x.shape[1]
    tc = max(t for t in (C // 2, C // 3, C // 6, C // 12) if t % V7X_LANES == 0 and L * t <= CONV_TILE_ELEMS)
    return pl.pallas_call(
        _conv_silu_kernel,
        grid=(bsz, C // tc),
        in_specs=[pl.BlockSpec((L, tc), lambda i, j: (row0 // L + i, j)),
                  pl.BlockSpec((SSM_CONV_W, tc), lambda i, j: (0, j)),
                  pl.BlockSpec((1, tc), lambda i, j: (0, j))],
        out_specs=pl.BlockSpec((1, L, tc), lambda i, j: (i, 0, j)),
        out_shape=jax.ShapeDtypeStruct((bsz, L, C), F32),
        scratch_shapes=[pltpu.VMEM((L + 2 * CONV_HALO, tc), F32)],
        compiler_params=_params("parallel", "parallel"),
    )(x, w, b.reshape(1, C))


def _gate_norm_kernel(yf_ref, yb_ref, z_ref, nw_ref, o_ref):
    z = z_ref[...]
    y = (yf_ref[...] + yb_ref[...]) * (z * jax.nn.sigmoid(z))
    for g in range(SSM_GROUPS):
        cols = slice(g * SSM_GROUP_W, (g + 1) * SSM_GROUP_W)
        seg = y[:, cols]
        ms = jnp.mean(jnp.square(seg), -1, keepdims=True)
        o_ref[:, cols] = (seg * lax.rsqrt(ms + SSM_NORM_EPS) * nw_ref[:, cols]).astype(o_ref.dtype)


def _gate_norm(y_f, y_b, z, row0, norm_w):
    n, c = y_f.shape
    tm = ROW_TILE
    return pl.pallas_call(
        _gate_norm_kernel,
        grid=(n // tm,),
        in_specs=[pl.BlockSpec((tm, c), lambda i: (i, 0)),
                  pl.BlockSpec((tm, c), lambda i: (i, 0)),
                  pl.BlockSpec((tm, c), lambda i: (row0 // tm + i, 0)),
                  pl.BlockSpec((1, c), lambda i: (0, 0))],
        out_specs=pl.BlockSpec((tm, c), lambda i: (i, 0)),
        out_shape=jax.ShapeDtypeStruct((n, c), BF16),
        compiler_params=_params("parallel"),
    )(y_f, y_b, z, norm_w.reshape(1, c))


def _ssd_kernel(xf_ref, xb_ref, bf_ref, bb_ref, cf_ref, cb_ref, dtf_ref, dtb_ref, dttf_ref, dttb_ref,
                a_ref, at_ref, dsk_ref, h0_ref, yf_ref, yb_ref, hl_ref):
    c = pl.program_id(2)
    Q = SSM_CHUNK
    P = SSM_HEAD_DIM

    @pl.when(c == 0)
    def _():
        hl_ref[...] = h0_ref[...]

    row = lax.broadcasted_iota(jnp.int32, (Q, Q), 0)
    col = lax.broadcasted_iota(jnp.int32, (Q, Q), 1)
    lower = (col <= row)
    upper = (col >= row)
    lower_f = lower.astype(F32)
    upper_f = upper.astype(F32)
    lane = lax.broadcasted_iota(jnp.int32, (Q, V7X_LANES), 1)
    lo_half = lane < P
    hi = lax.Precision.HIGHEST
    HPG, W = SSM_HPG, SSM_SPLIT_W
    hpg_bits, p_bits = HPG.bit_length() - 1, P.bit_length() - 1
    grp = lax.broadcasted_iota(jnp.int32, (Q, W), 1) >> hpg_bits
    sel_head = lax.broadcasted_iota(jnp.int32, (W, Q), 0) & (HPG - 1)
    exp_row = lax.broadcasted_iota(jnp.int32, (W, SSM_GROUP_W), 0)
    exp_col = lax.broadcasted_iota(jnp.int32, (W, SSM_GROUP_W), 1)
    expand = jnp.where((exp_row < 3 * HPG) & ((exp_row & (HPG - 1)) == (exp_col >> p_bits)),
                       1.0, 0.0).astype(BF16)

    def pieces(v):
        p1 = v.astype(BF16).astype(F32)
        r1 = v - p1
        p2 = r1.astype(BF16).astype(F32)
        return p1, p2, r1 - p2

    def split3(v, fill):
        p1, p2, p3 = pieces(v)
        return jnp.where(grp == 0, p1, jnp.where(grp == 1, p2, jnp.where(grp == 2, p3, fill))).astype(BF16)

    dirs = ((0, xf_ref, bf_ref, cf_ref, dtf_ref, dttf_ref, yf_ref, lower, lower_f, upper_f, Q - 1),
            (1, xb_ref, bb_ref, cb_ref, dtb_ref, dttb_ref, yb_ref, upper, upper_f, lower_f, 0))
    for (d, x_ref, b_ref, c_ref, dt_ref, dtt_ref, y_ref, valid, cum_mat, cum_mat_t, last) in dirs:
        dt = dt_ref[0, 0, 0]
        dtt = dtt_ref[0, 0, 0]
        a_row = a_ref[0, d]
        a_col = at_ref[0, d]
        acs = jnp.dot(cum_mat, dt * a_row, preferred_element_type=F32, precision=hi)
        acst = jnp.dot(dtt * a_col, cum_mat_t, preferred_element_type=F32, precision=hi)
        src_term = jnp.minimum(acst - jnp.log(dtt), -MASK_NEG)
        neg_mask = jnp.where(valid, 0.0, MASK_NEG)
        total = acs[last:last + 1, :]
        acs_sp = split3(acs, 1.0)
        scale_off = jnp.dot(split3(jnp.exp(acs), 0.0), expand, preferred_element_type=F32)
        w_end = jnp.dot(split3(jnp.exp(total - acs) * dt, 0.0), expand, preferred_element_type=F32)
        src_rows = jnp.concatenate([jnp.ones((3 * HPG, Q), F32)] + [-p for p in pieces(src_term)], axis=0)
        e_tot_t = jnp.exp(acst[:, last:last + 1])
        bmat = b_ref[0].astype(BF16)
        cmat = c_ref[0].astype(BF16)
        cbm = lax.dot_general(cmat, bmat, _NT, preferred_element_type=F32)
        cbm2 = jnp.concatenate([cbm, cbm], axis=1)
        neg_mask2 = jnp.concatenate([neg_mask, neg_mask], axis=1)
        h_prev = hl_ref[0, d]
        y_off = lax.dot_general(cmat, h_prev.astype(BF16), _NT, preferred_element_type=F32)
        xw_slabs = []
        for s in range(SSM_HPG // 2):
            h0, h1 = 2 * s, 2 * s + 1
            cols = slice(s * V7X_LANES, (s + 1) * V7X_LANES)
            x_slab = x_ref[0, :, cols]
            x_bf = x_slab.astype(BF16)
            sel = jnp.concatenate([jnp.where(sel_head == hh, src_rows, 0.0) for hh in (h0, h1)], axis=1)
            seg = jnp.dot(acs_sp, sel.astype(BF16), preferred_element_type=F32)
            m_pair = (cbm2 * jnp.exp(seg + neg_mask2)).astype(BF16)
            x_pair = [jnp.where(lo_half, x_bf, jnp.zeros_like(x_bf)), jnp.where(lo_half, jnp.zeros_like(x_bf), x_bf)]
            y_slab = jnp.dot(m_pair, jnp.concatenate(x_pair, axis=0), preferred_element_type=F32)
            skip = dsk_ref[0, d, :, cols]
            y_ref[0, :, cols] = y_slab + y_off[:, cols] * scale_off[:, cols] + skip * x_slab
            xw_slabs.append((x_slab * w_end[:, cols]).astype(BF16))
        xw = jnp.concatenate(xw_slabs, axis=-1)
        states = lax.dot_general(xw, bmat, (((0,), (0,)), ((), ())), preferred_element_type=F32)
        for hh in range(SSM_HPG):
            rows = slice(hh * P, (hh + 1) * P)
            hl_ref[0, d, rows, :] = h_prev[rows, :] * e_tot_t[hh:hh + 1, :] + states[rows, :]


def _ssd(xbc, dt, a, d_skip, h0):
    bsz, L, _ = xbc.shape
    nc = L // SSM_CHUNK
    G, HPG, Q, N = SSM_GROUPS, SSM_HPG, SSM_CHUNK, SSM_STATE
    gw = SSM_GROUP_W
    W = SSM_SPLIT_W
    dt_g = jnp.transpose(dt.reshape(bsz, L, 2, G, HPG), (2, 0, 3, 1, 4))
    dtt_g = jnp.transpose(dt_g, (0, 1, 2, 4, 3))
    a_g = jnp.transpose(a.reshape(2, G, 1, HPG), (1, 0, 2, 3))
    at_g = jnp.transpose(a_g, (0, 1, 3, 2))
    widen = lambda t: jnp.concatenate([t, t, t, jnp.zeros(t.shape[:-1] + (W - 3 * HPG,), t.dtype)], -1)
    dt_g, a_g = widen(dt_g), widen(a_g)
    dsk = jnp.repeat(d_skip.reshape(2, G, HPG), SSM_HEAD_DIM, axis=-1)
    dsk = jnp.transpose(dsk, (1, 0, 2)).reshape(G, 2, 1, gw)
    nb_x = SSM_D_INNER // N
    fwd = lambda c: c
    bwd = lambda c: nc - 1 - c
    x_spec = lambda f: pl.BlockSpec((1, Q, gw), lambda b, g, c: (b, f(c), g))
    b_spec = lambda f: pl.BlockSpec((1, Q, N), lambda b, g, c: (b, f(c), nb_x + g))
    c_spec = lambda f: pl.BlockSpec((1, Q, N), lambda b, g, c: (b, f(c), nb_x + G + g))
    dt_spec = lambda d, f: pl.BlockSpec((1, 1, 1, Q, W), lambda b, g, c: (d, b, g, f(c), 0))
    dtt_spec = lambda d, f: pl.BlockSpec((1, 1, 1, HPG, Q), lambda b, g, c: (d, b, g, 0, f(c)))
    y_spec = lambda f: pl.BlockSpec((1, Q, gw), lambda b, g, c: (b, f(c), g))
    h_spec = pl.BlockSpec((1, 2, gw, N), lambda b, g, c: (b, 0, g, 0))
    return pl.pallas_call(
        _ssd_kernel,
        grid=(bsz, G, nc),
        in_specs=[x_spec(fwd), x_spec(bwd), b_spec(fwd), b_spec(bwd), c_spec(fwd), c_spec(bwd),
                  dt_spec(0, fwd), dt_spec(1, bwd), dtt_spec(0, fwd), dtt_spec(1, bwd),
                  pl.BlockSpec((1, 2, 1, W), lambda b, g, c: (g, 0, 0, 0)),
                  pl.BlockSpec((1, 2, HPG, 1), lambda b, g, c: (g, 0, 0, 0)),
                  pl.BlockSpec((1, 2, 1, gw), lambda b, g, c: (g, 0, 0, 0)),
                  h_spec],
        out_specs=[y_spec(fwd), y_spec(bwd), h_spec],
        out_shape=[jax.ShapeDtypeStruct((bsz, L, SSM_D_INNER), F32),
                   jax.ShapeDtypeStruct((bsz, L, SSM_D_INNER), F32),
                   jax.ShapeDtypeStruct(h0.shape, F32)],
        compiler_params=_params("parallel", "parallel", "arbitrary"),
    )(xbc, xbc, xbc, xbc, xbc, xbc, dt_g, dt_g, dtt_g, dtt_g, a_g, at_g, dsk, h0)


def _rms_norm(x, g, eps=1e-6):
    return (x * lax.rsqrt(jnp.mean(jnp.square(x), -1, keepdims=True) + eps)) * g


def _rope_2d(x):
    L = x.shape[1]
    half = x.shape[-1] // 2
    quarter = half // 2
    t = jnp.arange(L)
    inv = ROPE_THETA ** (-jnp.arange(quarter, dtype=F32) / quarter)
    ang_r = (t // GRID_W).astype(F32)[:, None] * inv
    ang_c = (t % GRID_W).astype(F32)[:, None] * inv
    bshape = (L,) + (1,) * (x.ndim - 3) + (quarter,)

    def rot(v, ang):
        cos = jnp.cos(ang).reshape(bshape).astype(v.dtype)
        sin = jnp.sin(ang).reshape(bshape).astype(v.dtype)
        v1, v2 = v[..., :quarter], v[..., quarter:]
        return jnp.concatenate([v1 * cos - v2 * sin, v2 * cos + v1 * sin], -1)

    return jnp.concatenate([rot(x[..., :half], ang_r), rot(x[..., half:], ang_c)], -1)


def _mla_pack_q(q_nope, q_pe):
    b, L, h, _ = q_nope.shape
    pad = jnp.zeros((b, L, h, MLA_QK_PAD - MLA_NOPE - MLA_ROPE_DIM), BF16)
    return jnp.concatenate([q_nope.astype(BF16), q_pe.astype(BF16), pad], -1).reshape(b, L, h * MLA_QK_PAD)


def _mla_pack_k(k_nope, k_pe):
    b, L, h, _ = k_nope.shape
    kpe = jnp.broadcast_to(k_pe.astype(BF16)[:, :, None, :], (b, L, h, MLA_ROPE_DIM))
    pad = jnp.zeros((b, L, h, MLA_QK_PAD - MLA_NOPE - MLA_ROPE_DIM), BF16)
    return jnp.concatenate([k_nope.astype(BF16), kpe, pad], -1).reshape(b, L, h * MLA_QK_PAD)


def _ssm_mixer(h, bp, lp, bs, ls, h0_s, w_in, conv_w, conv_b, dt_bias, a_log, d_skip, norm_w):
    tp = bp * lp
    w_in = w_in.astype(BF16)
    z = _matmul(h, w_in[:, :SSM_D_INNER])
    xbc = _matmul(h, w_in[:, SSM_D_INNER:SSM_D_INNER + SSM_CONV_CH])
    dt_raw = _matmul(h, w_in[:, SSM_D_INNER + SSM_CONV_CH:])
    dt = jax.nn.softplus(dt_raw.reshape(-1, 2, SSM_HEADS) + dt_bias)
    a = -jnp.exp(a_log)
    acts = []
    st_p = None
    for (t0, bsz, L, h0) in ((0, bp, lp, None), (tp, bs, ls, h0_s)):
        n_tok = bsz * L
        xa = _conv_silu(xbc, t0, bsz, L, conv_w, conv_b)
        if h0 is None:
            h0 = jnp.zeros((bsz, 2, SSM_D_INNER, SSM_STATE), F32)
        else:
            h0 = h0.reshape(bsz, 2, SSM_D_INNER, SSM_STATE)
        y_f, y_b, h_last = _ssd(xa, dt[t0:t0 + n_tok].reshape(bsz, L, 2, SSM_HEADS), a, d_skip, h0)
        if st_p is None:
            st_p = h_last.reshape(bsz, 2, SSM_HEADS, SSM_HEAD_DIM, SSM_STATE)
        acts.append(_gate_norm(y_f.reshape(n_tok, SSM_D_INNER), y_b.reshape(n_tok, SSM_D_INNER), z, t0, norm_w))
    return jnp.concatenate(acts, 0), st_p


def _attn_mixer(h, bp, lp, bs, ls, c_k, c_v, c_ckv, c_kpe, w_in, rpb, q_norm, w_uq, kv_norm, w_ukv):
    tp, ts = bp * lp, bs * ls
    w_in = w_in.astype(BF16)
    o_qkv, o_cq, o_ckv = 3 * NA_QKV, 3 * NA_QKV + MLA_Q_LORA, 3 * NA_QKV + MLA_Q_LORA + MLA_KV_LORA
    qkv = _matmul(h, w_in[:, :o_qkv], out_dtype=BF16)
    kv_p32 = _matmul(h[:tp], w_in[:, NA_QKV:o_qkv])
    n_lat = w_in.shape[1] - o_qkv
    lat = _matmul(h, jnp.pad(w_in[:, o_qkv:], ((0, 0), (0, MLA_LATENT_PAD - n_lat))))
    cq, ckv, kpe = lat[:, :MLA_Q_LORA], lat[:, MLA_Q_LORA:o_ckv - o_qkv], lat[:, o_ckv - o_qkv:n_lat]
    ckv = _rms_norm(ckv, kv_norm, MLA_NORM_EPS)
    q_all = _matmul(_rms_norm(cq, q_norm, MLA_NORM_EPS).astype(BF16), w_uq)
    lc = c_ckv.shape[1]
    kv_all = _matmul(jnp.concatenate([ckv, c_ckv.reshape(-1, MLA_KV_LORA)], 0).astype(BF16), w_ukv)

    hq = MLA_NOPE + MLA_ROPE_DIM
    hkv = MLA_NOPE + MLA_V
    mla_scale = hq ** -0.5
    mla = functools.partial(_mha, heads=MLA_HEADS, dk=MLA_QK_PAD, dv=MLA_V, scale=mla_scale)

    o_na_p = _mha(qkv, qkv, qkv, bp, lp, lp, NA_HEADS, NA_HEAD_DIM, NA_HEAD_DIM, NA_HEAD_DIM ** -0.5,
                  q_col=0, k_col=1, v_col=2)
    q_p = q_all[:tp].reshape(bp, lp, MLA_HEADS, hq)
    kv_p = kv_all[:tp].reshape(bp, lp, MLA_HEADS, hkv)
    o_mla_p = mla(_mla_pack_q(q_p[..., :MLA_NOPE], q_p[..., MLA_NOPE:]).reshape(tp, -1),
                  _mla_pack_k(kv_p[..., :MLA_NOPE], kpe[:tp].reshape(bp, lp, -1)).reshape(tp, -1),
                  kv_p[..., MLA_NOPE:].astype(BF16).reshape(tp, -1), bsz=bp, lq=lp, lk=lp)

    o_na_s = _na_latent(qkv, tp, bs, ls, c_k.astype(BF16).reshape(bs, -1, NA_QKV),
                        c_v.astype(BF16).reshape(bs, -1, NA_QKV), rpb)
    q_s = q_all[tp:].reshape(bs, ls, MLA_HEADS, hq)
    kv_s = jnp.concatenate([kv_all[tp:tp + ts].reshape(bs, ls, MLA_HEADS, hkv),
                            kv_all[tp + ts:].reshape(bs, lc, MLA_HEADS, hkv)], 1)
    kpe_s = jnp.concatenate([_rope_2d(kpe[tp:].reshape(bs, ls, -1)), c_kpe], 1)
    o_mla_s = mla(_mla_pack_q(q_s[..., :MLA_NOPE], _rope_2d(q_s[..., MLA_NOPE:])).reshape(ts, -1),
                  _mla_pack_k(kv_s[..., :MLA_NOPE], kpe_s).reshape(bs * (ls + lc), -1),
                  kv_s[..., MLA_NOPE:].astype(BF16).reshape(bs * (ls + lc), -1), bsz=bs, lq=ls, lk=ls + lc)

    o_cat = jnp.concatenate([jnp.concatenate([o_na_p, o_na_s], 0), jnp.concatenate([o_mla_p, o_mla_s], 0)], -1)
    new = (kv_p32[:, :NA_QKV].reshape(bp, lp, NA_HEADS, NA_HEAD_DIM),
           kv_p32[:, NA_QKV:].reshape(bp, lp, NA_HEADS, NA_HEAD_DIM),
           ckv[:tp].reshape(bp, lp, -1), kpe[:tp].reshape(bp, lp, -1))
    return o_cat, new


def _modulation(cvec, w, b):
    return jnp.split(jax.nn.silu(cvec) @ w + b, 6, axis=-1)


def kernel(x_prompt, x_sample, c, cache_na_k, cache_na_v, cache_mla_ckv, cache_mla_kpe, state_ssm,
           c_ctx, w_ada, b_ada, ln_g, ln_b, attn_w_in, na_rpb, mla_q_norm, mla_w_uq, mla_kv_norm,
           mla_w_ukv, attn_w_out, ssm_w_in, ssm_conv_w, ssm_conv_b, ssm_dt_bias, ssm_a_log, ssm_d,
           ssm_norm, ssm_w_out, moe_w_router, moe_b_router, moe_w_gu, moe_b_gu, moe_w_down, moe_b_down):
    bp, lp, D = x_prompt.shape
    bs, ls, _ = x_sample.shape
    tp = bp * lp
    tiles_p, tiles_per_s = tp // ROW_TILE, ls // ROW_TILE
    new_k, new_v, new_ckv, new_kpe, new_ssm = [], [], [], [], []

    cond = jnp.concatenate([c_ctx[None, :], c], 0)
    mods = [_modulation(cond, w_ada[l], b_ada[l]) for l in range(DEPTH)]
    zero = jnp.zeros_like(mods[0][0])

    x = jnp.concatenate([x_prompt.reshape(tp, D), x_sample.reshape(bs * ls, D)], 0)
    shift1, scale1 = mods[0][0], mods[0][1]
    h = jnp.concatenate(
        [(x_prompt * (1.0 + scale1[0]) + shift1[0]).reshape(tp, D),
         (x_sample * (1.0 + scale1[1:, None, :]) + shift1[1:, None, :]).reshape(bs * ls, D)], 0).astype(BF16)

    for l in range(DEPTH):
        m = mods[l]
        if l % 2 == 0:
            i = l // 2
            act, (k_c, v_c, ckv_c, kpe_c) = _attn_mixer(
                h, bp, lp, bs, ls, cache_na_k[:, i], cache_na_v[:, i], cache_mla_ckv[:, i], cache_mla_kpe[:, i],
                attn_w_in[i], na_rpb[i], mla_q_norm[i], mla_w_uq[i], mla_kv_norm[i], mla_w_ukv[i])
            w_out = attn_w_out[i]
            new_k.append(k_c)
            new_v.append(v_c)
            new_ckv.append(ckv_c)
            new_kpe.append(kpe_c)
        else:
            j = l // 2
            act, st = _ssm_mixer(h, bp, lp, bs, ls, state_ssm[:, j], ssm_w_in[j], ssm_conv_w[j], ssm_conv_b[j],
                                 ssm_dt_bias[j], ssm_a_log[j], ssm_d[j], ssm_norm[j])
            w_out = ssm_w_out[j]
            new_ssm.append(st)
        x1, hm, logits = _proj_res_ln(act, w_out, x, _mod_table(m[2], m[3], m[4]),
                                      jnp.stack([ln_g[l, 0], ln_b[l, 0]]), moe_w_router[l], moe_b_router[l],
                                      tiles_p, tiles_per_s)
        yg, gates = _moe(hm, logits, _cast_bf16(moe_w_gu, l), moe_b_gu[l], _cast_bf16(moe_w_down, l),
                         moe_b_down[l])
        nxt = mods[l + 1] if l + 1 < DEPTH else (zero, zero)
        x, h = _combine_ln(yg, gates, x1, _mod_table(m[5], nxt[0], nxt[1]),
                           jnp.stack([ln_g[l, 1], ln_b[l, 1]]), tiles_p, tiles_per_s)
    return (x[:tp].reshape(bp, lp, D), x[tp:].reshape(bs, ls, D), jnp.stack(new_k, 1), jnp.stack(new_v, 1),
            jnp.stack(new_ckv, 1), jnp.stack(new_kpe, 1), jnp.stack(new_ssm, 1))
```

```python
import functools
import math

import jax
import jax.numpy as jnp
import numpy as np
from jax import lax
from jax.experimental import pallas as pl
from jax.experimental.pallas import tpu as pltpu

D_MODEL = 2048
DEPTH = 4
GRID_W = 64

NA_HEADS = 8
NA_HEAD_DIM = 128
NA_WR = 8
NA_WC = 16

MLA_HEADS = 8
MLA_Q_LORA = 768
MLA_KV_LORA = 512
MLA_NOPE = 128
MLA_ROPE_DIM = 64
MLA_V = 128
MLA_QK_PAD = 256
MLA_LATENT_PAD = 1536
ROPE_THETA = 10000.0

NA_QKV = NA_HEADS * NA_HEAD_DIM

SSM_D_INNER = 2 * D_MODEL
SSM_HEAD_DIM = 64
SSM_HEADS = SSM_D_INNER // SSM_HEAD_DIM
SSM_GROUPS = 8
SSM_HPG = SSM_HEADS // SSM_GROUPS
SSM_GROUP_W = SSM_HPG * SSM_HEAD_DIM
SSM_STATE = 128
SSM_SPLIT_W = 6 * SSM_HPG
SSM_CONV_W = 5
SSM_CHUNK = 128
SSM_CONV_CH = SSM_D_INNER + 2 * SSM_GROUPS * SSM_STATE

N_EXPERTS = 32
TOP_K = 4
D_FF = D_MODEL // 2
SWIGLU_LIMIT = 7.0
SWIGLU_ALPHA = 1.702

DEEPNORM_ALPHA = (2.0 * DEPTH) ** 0.25
LN_EPS = 1e-5
MASK_NEG = -1e30
SSM_NORM_EPS = 1e-5
MLA_NORM_EPS = 1e-6

V7X_VMEM_LIMIT_BYTES = 48 * 1024 * 1024
V7X_LANES = 128
V7X_SUBLANES = 8
ROW_TILE = 256
MM_TILE_M = 1024
MM_TILE_N = 1024
MOE_TILE_M = 512
CAST_TILE_K = 512
ATTN_TILE_Q = 256
NA_ROWS_PER_STEP = 2
CONV_TILE_ELEMS = 1024 * 1024
CONV_HALO = 8

BF16 = jnp.bfloat16
F32 = jnp.float32
_NT = (((1,), (1,)), ((), ()))


def _params(*sem):
    return pltpu.CompilerParams(dimension_semantics=sem, vmem_limit_bytes=V7X_VMEM_LIMIT_BYTES)


def _mm_kernel(x_ref, w_ref, o_ref):
    o_ref[...] = jnp.dot(x_ref[...], w_ref[...], preferred_element_type=F32).astype(o_ref.dtype)


def _pick_tile(n, pref):
    for t in (pref, 768, 512, 384, 256, 128):
        if t <= pref and n % t == 0:
            return t
    return n


def _matmul(x, w, out_dtype=F32):
    m, k = x.shape
    n = w.shape[1]
    tm = _pick_tile(m, MM_TILE_M)
    tn = _pick_tile(n, MM_TILE_N)
    return pl.pallas_call(
        _mm_kernel,
        grid=(m // tm, n // tn),
        in_specs=[pl.BlockSpec((tm, k), lambda i, j: (i, 0)),
                  pl.BlockSpec((k, tn), lambda i, j: (0, j))],
        out_specs=pl.BlockSpec((tm, tn), lambda i, j: (i, j)),
        out_shape=jax.ShapeDtypeStruct((m, n), out_dtype),
        compiler_params=_params("parallel", "parallel"),
    )(x.astype(BF16), w.astype(BF16))


def _cast_kernel(x_ref, o_ref):
    o_ref[...] = x_ref[0].astype(o_ref.dtype)


def _cast_bf16(w, l):
    _, e, k, n = w.shape
    tk = _pick_tile(k, CAST_TILE_K)
    return pl.pallas_call(
        _cast_kernel,
        grid=(e, k // tk),
        in_specs=[pl.BlockSpec((1, 1, tk, n), lambda i, j: (l, i, j, 0))],
        out_specs=pl.BlockSpec((1, tk, n), lambda i, j: (i, j, 0)),
        out_shape=jax.ShapeDtypeStruct((e, k, n), BF16),
        compiler_params=_params("parallel", "parallel"),
    )(w)


def _req_of_tile(i, tiles_p, tiles_per_s):
    return jnp.where(i < tiles_p, 0, 1 + (i - tiles_p) // tiles_per_s)


def _mod_table(*vecs):
    r, d = vecs[0].shape
    pad = [jnp.zeros((r, d), F32)] * (V7X_SUBLANES - len(vecs))
    return jnp.stack(list(vecs) + pad, axis=1)


def _ln_mod(u, g, b, shift, scale):
    mu = jnp.mean(u, -1, keepdims=True)
    var = jnp.mean(jnp.square(u - mu), -1, keepdims=True)
    xn = (u - mu) * lax.rsqrt(var + LN_EPS) * g + b
    return xn, xn * (1.0 + scale) + shift


def _proj_res_ln_kernel(a_ref, w_ref, x_ref, mod_ref, ln_ref, wrh_ref, wrl_ref, br_ref,
                        x1_ref, hm_ref, te_ref, gt_ref):
    y = jnp.dot(a_ref[...], w_ref[...], preferred_element_type=F32)
    u = DEEPNORM_ALPHA * x_ref[...] + mod_ref[0, 0:1, :] * y
    x1, hm = _ln_mod(u, ln_ref[0:1, :], ln_ref[1:2, :], mod_ref[0, 1:2, :], mod_ref[0, 2:3, :])
    x1_ref[...] = x1
    hm_hi = hm.astype(BF16)
    hm_ref[...] = hm_hi
    hm_lo = (hm - hm_hi.astype(F32)).astype(BF16)
    logits = (jnp.dot(hm_hi, wrh_ref[...], preferred_element_type=F32)
              + jnp.dot(hm_hi, wrl_ref[...], preferred_element_type=F32)
              + jnp.dot(hm_lo, wrh_ref[...], preferred_element_type=F32) + br_ref[...])
    ne = logits.shape[1]
    expert = lax.broadcasted_iota(jnp.int32, logits.shape, 1).astype(F32)
    slot = lax.broadcasted_iota(jnp.int32, te_ref.shape, 1)
    top_v = jnp.zeros(te_ref.shape, F32)
    top_e = jnp.zeros(te_ref.shape, F32)
    for kk in range(TOP_K):
        best = jnp.max(logits, -1, keepdims=True)
        best_e = jnp.min(jnp.where(logits == best, expert, float(ne)), -1, keepdims=True)
        top_v = jnp.where(slot == kk, best, top_v)
        top_e = jnp.where(slot == kk, best_e, top_e)
        logits = jnp.where(expert == best_e, -jnp.inf, logits)
    p = jnp.exp(top_v - jnp.max(top_v, -1, keepdims=True))
    te_ref[...] = top_e.astype(jnp.int32)
    gt_ref[...] = p / jnp.sum(p, -1, keepdims=True)


def _proj_res_ln(a, w, x, mod, ln, w_router, b_router, tiles_p, tiles_per_s):
    t, k = a.shape
    d = w.shape[1]
    ne = w_router.shape[1]
    tm = ROW_TILE
    req = lambda i: (_req_of_tile(i, tiles_p, tiles_per_s), 0, 0)
    wr_hi = w_router.astype(BF16)
    wr_lo = (w_router - wr_hi.astype(F32)).astype(BF16)
    return pl.pallas_call(
        _proj_res_ln_kernel,
        grid=(t // tm,),
        in_specs=[pl.BlockSpec((tm, k), lambda i: (i, 0)),
                  pl.BlockSpec((k, d), lambda i: (0, 0), pipeline_mode=pl.Buffered(1)),
                  pl.BlockSpec((tm, d), lambda i: (i, 0)),
                  pl.BlockSpec((1, V7X_SUBLANES, d), req),
                  pl.BlockSpec((2, d), lambda i: (0, 0)),
                  pl.BlockSpec((d, ne), lambda i: (0, 0)),
                  pl.BlockSpec((d, ne), lambda i: (0, 0)),
                  pl.BlockSpec((1, ne), lambda i: (0, 0))],
        out_specs=[pl.BlockSpec((tm, d), lambda i: (i, 0)),
                   pl.BlockSpec((tm, d), lambda i: (i, 0)),
                   pl.BlockSpec((tm, TOP_K), lambda i: (i, 0)),
                   pl.BlockSpec((tm, TOP_K), lambda i: (i, 0))],
        out_shape=[jax.ShapeDtypeStruct((t, d), F32), jax.ShapeDtypeStruct((t, d), BF16),
                   jax.ShapeDtypeStruct((t, TOP_K), jnp.int32), jax.ShapeDtypeStruct((t, TOP_K), F32)],
        compiler_params=_params("parallel"),
    )(a, w.astype(BF16), x, mod, ln, wr_hi, wr_lo, b_router.reshape(1, ne))


def _combine_ln_kernel(*refs):
    y_refs = refs[:TOP_K]
    gt_ref, x_ref, mod_ref, ln_ref, x2_ref, h_ref = refs[TOP_K:]
    gt = gt_ref[...]
    ym = gt[:, 0:1] * y_refs[0][...].astype(F32)
    for kk in range(1, TOP_K):
        ym = ym + gt[:, kk:kk + 1] * y_refs[kk][...].astype(F32)
    u = DEEPNORM_ALPHA * x_ref[...] + mod_ref[0, 0:1, :] * ym
    x2, h = _ln_mod(u, ln_ref[0:1, :], ln_ref[1:2, :], mod_ref[0, 1:2, :], mod_ref[0, 2:3, :])
    x2_ref[...] = x2
    h_ref[...] = h.astype(h_ref.dtype)


def _combine_ln(yg, gates, x, mod, ln, tiles_p, tiles_per_s):
    t, d = x.shape
    tm = ROW_TILE
    nt = t // tm
    req = lambda i: (_req_of_tile(i, tiles_p, tiles_per_s), 0, 0)
    y_specs = [pl.BlockSpec((tm, d), functools.partial(lambda i, kk: (kk * nt + i, 0), kk=kk))
               for kk in range(TOP_K)]
    return pl.pallas_call(
        _combine_ln_kernel,
        grid=(nt,),
        in_specs=y_specs + [pl.BlockSpec((tm, TOP_K), lambda i: (i, 0)),
                            pl.BlockSpec((tm, d), lambda i: (i, 0)),
                            pl.BlockSpec((1, V7X_SUBLANES, d), req),
                            pl.BlockSpec((2, d), lambda i: (0, 0))],
        out_specs=[pl.BlockSpec((tm, d), lambda i: (i, 0)),
                   pl.BlockSpec((tm, d), lambda i: (i, 0))],
        out_shape=[jax.ShapeDtypeStruct((t, d), F32), jax.ShapeDtypeStruct((t, d), BF16)],
        compiler_params=_params("parallel"),
    )(*([yg] * TOP_K), gates, x, mod, ln)


def _moe_ffn_kernel(be_ref, nused_ref, x_ref, wgu_ref, bgu_ref, wd_ref, bd_ref, o_ref):
    @pl.when(pl.program_id(0) < nused_ref[0])
    def _():
        x = x_ref[...]
        gu = jnp.dot(x, wgu_ref[0], preferred_element_type=F32) + bgu_ref[0]
        x_glu = jnp.minimum(gu[:, :D_FF], SWIGLU_LIMIT)
        x_lin = jnp.clip(gu[:, D_FF:], -SWIGLU_LIMIT, SWIGLU_LIMIT)
        act = x_glu * jax.nn.sigmoid(SWIGLU_ALPHA * x_glu) * (x_lin + 1.0)
        y = jnp.dot(act.astype(BF16), wd_ref[0], preferred_element_type=F32) + bd_ref[0]
        o_ref[...] = y.astype(o_ref.dtype)

    @pl.when(pl.program_id(0) >= nused_ref[0])
    def _():
        o_ref[...] = jnp.zeros_like(o_ref)


def _moe_ffn(xb, block_e, n_used, w_gu, b_gu, w_down, b_down):
    rows, d = xb.shape
    nb = rows // MOE_TILE_M
    grid_spec = pltpu.PrefetchScalarGridSpec(
        num_scalar_prefetch=2,
        grid=(nb,),
        in_specs=[pl.BlockSpec((MOE_TILE_M, d), lambda i, be, nu: (i, 0)),
                  pl.BlockSpec((1, d, 2 * D_FF), lambda i, be, nu: (be[i], 0, 0)),
                  pl.BlockSpec((1, 1, 2 * D_FF), lambda i, be, nu: (be[i], 0, 0)),
                  pl.BlockSpec((1, D_FF, d), lambda i, be, nu: (be[i], 0, 0)),
                  pl.BlockSpec((1, 1, d), lambda i, be, nu: (be[i], 0, 0))],
        out_specs=pl.BlockSpec((MOE_TILE_M, d), lambda i, be, nu: (i, 0)),
    )
    return pl.pallas_call(
        _moe_ffn_kernel,
        grid_spec=grid_spec,
        out_shape=jax.ShapeDtypeStruct((rows, d), BF16),
        compiler_params=_params("arbitrary"),
    )(block_e, n_used, xb, w_gu, b_gu.reshape(N_EXPERTS, 1, 2 * D_FF), w_down,
      b_down.reshape(N_EXPERTS, 1, d))


def _moe(hm, top_e, w_gu, b_gu, w_down, b_down):
    T, d = hm.shape
    tm = MOE_TILE_M
    n = T * TOP_K
    flat_e = top_e.reshape(-1)
    iota_n = jnp.arange(n, dtype=jnp.int32)
    sorted_e, order = lax.sort((flat_e, iota_n), num_keys=1, is_stable=True)
    bounds = jnp.searchsorted(sorted_e, jnp.arange(N_EXPERTS + 1, dtype=jnp.int32), side='left').astype(jnp.int32)
    start, counts = bounds[:-1], bounds[1:] - bounds[:-1]
    padded = (counts + tm - 1) // tm * tm
    pad_end = jnp.cumsum(padded)
    pad_start = pad_end - padded
    pos_sorted = pad_start[sorted_e] + iota_n - start[sorted_e]
    _, pos = lax.sort((order, pos_sorted), num_keys=1)
    nb = n // tm + N_EXPERTS
    n_used = (pad_end[-1] // tm).astype(jnp.int32)
    blk = jnp.arange(nb, dtype=jnp.int32)
    block_e = jnp.minimum(jnp.searchsorted(pad_end, blk * tm, side='right'), N_EXPERTS - 1).astype(jnp.int32)
    last_e = block_e[jnp.maximum(n_used - 1, 0)]
    block_e = jnp.where(blk < n_used, block_e, last_e)
    r = jnp.arange(nb * tm, dtype=jnp.int32)
    row_e = jnp.repeat(block_e, tm)
    off = r - pad_start[row_e]
    valid = (off < counts[row_e]) & (r < pad_end[-1])
    src = jnp.clip(start[row_e] + off, 0, n - 1)
    row_tok = jnp.where(valid, order[src] // TOP_K, 0)
    xb = hm[row_tok]
    yb = _moe_ffn(xb, block_e, n_used.reshape(1), w_gu, b_gu, w_down, b_down)
    pos_kt = jnp.transpose(pos.reshape(T, TOP_K)).reshape(-1)
    return yb[pos_kt]


def _softmax_pv(s_list, v_list):
    m = functools.reduce(jnp.maximum, [jnp.max(s, -1, keepdims=True) for s in s_list])
    p_list = [jnp.exp(s - m) for s in s_list]
    denom = functools.reduce(jnp.add, [jnp.sum(p, -1, keepdims=True) for p in p_list])
    o = functools.reduce(jnp.add, [jnp.dot(p.astype(BF16), v, preferred_element_type=F32)
                                   for p, v in zip(p_list, v_list)])
    return o / denom


def _mha_kernel(q_ref, k_ref, v_ref, o_ref, *, heads, dk, dv, scale):
    for h in range(heads):
        q = q_ref[:, h * dk:(h + 1) * dk]
        k = k_ref[:, h * dk:(h + 1) * dk]
        s = lax.dot_general(q, k, _NT, preferred_element_type=F32) * scale
        o = _softmax_pv([s], [v_ref[:, h * dv:(h + 1) * dv]])
        o_ref[:, h * dv:(h + 1) * dv] = o.astype(o_ref.dtype)


def _mha(q, k, v, bsz, lq, lk, heads, dk, dv, scale, q_col=0, k_col=0, v_col=0):
    tq = min(ATTN_TILE_Q, lq)
    nq = lq // tq
    return pl.pallas_call(
        functools.partial(_mha_kernel, heads=heads, dk=dk, dv=dv, scale=scale),
        grid=(bsz, nq),
        in_specs=[pl.BlockSpec((tq, heads * dk), lambda i, j: (i * nq + j, q_col)),
                  pl.BlockSpec((lk, heads * dk), lambda i, j: (i, k_col)),
                  pl.BlockSpec((lk, heads * dv), lambda i, j: (i, v_col))],
        out_specs=pl.BlockSpec((tq, heads * dv), lambda i, j: (i * nq + j, 0)),
        out_shape=jax.ShapeDtypeStruct((bsz * lq, heads * dv), BF16),
        compiler_params=_params("parallel", "arbitrary"),
    )(q, k, v)


def _na_kernel(q_ref, k_ref, v_ref, kc_ref, vc_ref, bias_ref, o_ref, *, heads, d, scale, grid_rows, win):
    rows_per_step = bias_ref.shape[0]
    for rr in range(rows_per_step):
        r = pl.program_id(1) * rows_per_step + rr
        rs = jnp.clip(r - NA_WR // 2, 0, grid_rows - win)
        start = pl.multiple_of(rs * GRID_W, GRID_W)
        for h in range(heads):
            hs = slice(h * d, (h + 1) * d)
            q = q_ref[rr * GRID_W:(rr + 1) * GRID_W, hs]
            k_loc = k_ref[pl.ds(start, win * GRID_W), hs]
            v_loc = v_ref[pl.ds(start, win * GRID_W), hs]
            s_loc = lax.dot_general(q, k_loc, _NT, preferred_element_type=F32) * scale + bias_ref[rr, h]
            s_ctx = lax.dot_general(q, kc_ref[0, :, hs], _NT, preferred_element_type=F32) * scale
            o = _softmax_pv([s_loc, s_ctx], [v_loc, vc_ref[0, :, hs]])
            o_ref[rr * GRID_W:(rr + 1) * GRID_W, hs] = o.astype(o_ref.dtype)


def _na_bias(rpb, grid_rows, win):
    qc = np.arange(GRID_W)
    kc = np.arange(GRID_W)
    qcs = np.clip(qc - NA_WC // 2, 0, GRID_W - NA_WC)
    col_ok = (kc[None, :] >= qcs[:, None]) & (kc[None, :] < qcs[:, None] + NA_WC)
    col_idx = np.clip(kc[None, :] - qc[:, None] + NA_WC - 1, 0, 2 * NA_WC - 2)
    r = np.arange(grid_rows)
    rs = np.clip(r - NA_WR // 2, 0, grid_rows - win)
    row_idx = rs[:, None] + np.arange(win)[None, :] - r[:, None] + NA_WR - 1
    bias = rpb[:, row_idx][:, :, :, col_idx]
    bias = jnp.where(jnp.asarray(col_ok)[None, None, None], bias.astype(F32), MASK_NEG)
    bias = jnp.transpose(bias, (1, 0, 3, 2, 4))
    return bias.reshape(grid_rows, rpb.shape[0], GRID_W, win * GRID_W)


def _na_latent(qkv, row0, bsz, L, k_ctx, v_ctx, rpb):
    heads = rpb.shape[0]
    hd = qkv.shape[1] // 3
    d = hd // heads
    lc = k_ctx.shape[1]
    grid_rows = L // GRID_W
    win = min(NA_WR, grid_rows)
    rps = NA_ROWS_PER_STEP
    tq = rps * GRID_W
    nq = L // tq
    bias = _na_bias(rpb, grid_rows, win)
    return pl.pallas_call(
        functools.partial(_na_kernel, heads=heads, d=d, scale=d ** -0.5, grid_rows=grid_rows, win=win),
        grid=(bsz, nq),
        in_specs=[pl.BlockSpec((tq, hd), lambda i, j: (row0 // tq + i * nq + j, 0)),
                  pl.BlockSpec((L, hd), lambda i, j: (row0 // L + i, 1)),
                  pl.BlockSpec((L, hd), lambda i, j: (row0 // L + i, 2)),
                  pl.BlockSpec((1, lc, hd), lambda i, j: (i, 0, 0)),
                  pl.BlockSpec((1, lc, hd), lambda i, j: (i, 0, 0)),
                  pl.BlockSpec((rps, heads, GRID_W, win * GRID_W), lambda i, j: (j, 0, 0, 0))],
        out_specs=pl.BlockSpec((tq, hd), lambda i, j: (i * nq + j, 0)),
        out_shape=jax.ShapeDtypeStruct((bsz * L, hd), BF16),
        compiler_params=_params("parallel", "arbitrary"),
    )(qkv, qkv, qkv, k_ctx, v_ctx, bias)


def _conv_silu_kernel(x_ref, w_ref, b_ref, o_ref, pad_ref):
    L = x_ref.shape[0]
    zeros = jnp.zeros((CONV_HALO, pad_ref.shape[1]), F32)
    pad_ref[0:CONV_HALO, :] = zeros
    pad_ref[CONV_HALO + L:, :] = zeros
    pad_ref[CONV_HALO:CONV_HALO + L, :] = x_ref[...]
    half = SSM_CONV_W // 2
    acc = b_ref[...] + w_ref[0:1, :] * pad_ref[CONV_HALO - half:CONV_HALO - half + L, :]
    for t in range(1, SSM_CONV_W):
        off = CONV_HALO - half + t
        acc = acc + w_ref[t:t + 1, :] * pad_ref[off:off + L, :]
    o_ref[0] = acc * jax.nn.sigmoid(acc)


def _conv_silu(x, row0, bsz, L, w, b):
    C = x.shape[1]
    tc = max(t for t in (C // 2, C // 3, C // 6, C // 12) if t % V7X_LANES == 0 and L * t <= CONV_TILE_ELEMS)
    return pl.pallas_call(
        _conv_silu_kernel,
        grid=(bsz, C // tc),
        in_specs=[pl.BlockSpec((L, tc), lambda i, j: (row0 // L + i, j)),
                  pl.BlockSpec((SSM_CONV_W, tc), lambda i, j: (0, j)),
                  pl.BlockSpec((1, tc), lambda i, j: (0, j))],
        out_specs=pl.BlockSpec((1, L, tc), lambda i, j: (i, 0, j)),
        out_shape=jax.ShapeDtypeStruct((bsz, L, C), F32),
        scratch_shapes=[pltpu.VMEM((L + 2 * CONV_HALO, tc), F32)],
        compiler_params=_params("parallel", "parallel"),
    )(x, w, b.reshape(1, C))


def _gate_norm_kernel(yf_ref, yb_ref, z_ref, nw_ref, o_ref):
    z = z_ref[...]
    y = (yf_ref[...] + yb_ref[...]) * (z * jax.nn.sigmoid(z))
    for g in range(SSM_GROUPS):
        cols = slice(g * SSM_GROUP_W, (g + 1) * SSM_GROUP_W)
        seg = y[:, cols]
        ms = jnp.mean(jnp.square(seg), -1, keepdims=True)
        o_ref[:, cols] = (seg * lax.rsqrt(ms + SSM_NORM_EPS) * nw_ref[:, cols]).astype(o_ref.dtype)


def _gate_norm(y_f, y_b, z, row0, norm_w):
    n, c = y_f.shape
    tm = ROW_TILE
    return pl.pallas_call(
        _gate_norm_kernel,
        grid=(n // tm,),
        in_specs=[pl.BlockSpec((tm, c), lambda i: (i, 0)),
                  pl.BlockSpec((tm, c), lambda i: (i, 0)),
                  pl.BlockSpec((tm, c), lambda i: (row0 // tm + i, 0)),
                  pl.BlockSpec((1, c), lambda i: (0, 0))],
        out_specs=pl.BlockSpec((tm, c), lambda i: (i, 0)),
        out_shape=jax.ShapeDtypeStruct((n, c), BF16),
        compiler_params=_params("parallel"),
    )(y_f, y_b, z, norm_w.reshape(1, c))


def _ssd_kernel(xf_ref, xb_ref, bf_ref, bb_ref, cf_ref, cb_ref, dtf_ref, dtb_ref, dttf_ref, dttb_ref,
                a_ref, at_ref, dsk_ref, h0_ref, yf_ref, yb_ref, hl_ref):
    c = pl.program_id(2)
    Q = SSM_CHUNK
    P = SSM_HEAD_DIM

    @pl.when(c == 0)
    def _():
        hl_ref[...] = h0_ref[...]

    row = lax.broadcasted_iota(jnp.int32, (Q, Q), 0)
    col = lax.broadcasted_iota(jnp.int32, (Q, Q), 1)
    lower = (col <= row)
    upper = (col >= row)
    lower_f = lower.astype(F32)
    upper_f = upper.astype(F32)
    lane = lax.broadcasted_iota(jnp.int32, (Q, V7X_LANES), 1)
    lo_half = lane < P
    hi = lax.Precision.HIGHEST
    HPG, W = SSM_HPG, SSM_SPLIT_W
    hpg_bits, p_bits = HPG.bit_length() - 1, P.bit_length() - 1
    grp = lax.broadcasted_iota(jnp.int32, (Q, W), 1) >> hpg_bits
    sel_head = lax.broadcasted_iota(jnp.int32, (W, Q), 0) & (HPG - 1)
    exp_row = lax.broadcasted_iota(jnp.int32, (W, SSM_GROUP_W), 0)
    exp_col = lax.broadcasted_iota(jnp.int32, (W, SSM_GROUP_W), 1)
    expand = jnp.where((exp_row < 3 * HPG) & ((exp_row & (HPG - 1)) == (exp_col >> p_bits)),
                       1.0, 0.0).astype(BF16)

    def pieces(v):
        p1 = v.astype(BF16).astype(F32)
        r1 = v - p1
        p2 = r1.astype(BF16).astype(F32)
        return p1, p2, r1 - p2

    def split3(v, fill):
        p1, p2, p3 = pieces(v)
        return jnp.where(grp == 0, p1, jnp.where(grp == 1, p2, jnp.where(grp == 2, p3, fill))).astype(BF16)

    dirs = ((0, xf_ref, bf_ref, cf_ref, dtf_ref, dttf_ref, yf_ref, lower, lower_f, upper_f, Q - 1),
            (1, xb_ref, bb_ref, cb_ref, dtb_ref, dttb_ref, yb_ref, upper, upper_f, lower_f, 0))
    for (d, x_ref, b_ref, c_ref, dt_ref, dtt_ref, y_ref, valid, cum_mat, cum_mat_t, last) in dirs:
        dt = dt_ref[0, 0, 0]
        dtt = dtt_ref[0, 0, 0]
        a_row = a_ref[0, d]
        a_col = at_ref[0, d]
        acs = jnp.dot(cum_mat, dt * a_row, preferred_element_type=F32, precision=hi)
        acst = jnp.dot(dtt * a_col, cum_mat_t, preferred_element_type=F32, precision=hi)
        src_term = jnp.minimum(acst - jnp.log(dtt), -MASK_NEG)
        neg_mask = jnp.where(valid, 0.0, MASK_NEG)
        total = acs[last:last + 1, :]
        acs_sp = split3(acs, 1.0)
        scale_off = jnp.dot(split3(jnp.exp(acs), 0.0), expand, preferred_element_type=F32)
        w_end = jnp.dot(split3(jnp.exp(total - acs) * dt, 0.0), expand, preferred_element_type=F32)
        src_rows = jnp.concatenate([jnp.ones((3 * HPG, Q), F32)] + [-p for p in pieces(src_term)], axis=0)
        e_tot_t = jnp.exp(acst[:, last:last + 1])
        bmat = b_ref[0].astype(BF16)
        cmat = c_ref[0].astype(BF16)
        cbm = lax.dot_general(cmat, bmat, _NT, preferred_element_type=F32)
        cbm2 = jnp.concatenate([cbm, cbm], axis=1)
        neg_mask2 = jnp.concatenate([neg_mask, neg_mask], axis=1)
        h_prev = hl_ref[0, d]
        y_off = lax.dot_general(cmat, h_prev.astype(BF16), _NT, preferred_element_type=F32)
        xw_slabs = []
        for s in range(SSM_HPG // 2):
            h0, h1 = 2 * s, 2 * s + 1
            cols = slice(s * V7X_LANES, (s + 1) * V7X_LANES)
            x_slab = x_ref[0, :, cols]
            x_bf = x_slab.astype(BF16)
            sel = jnp.concatenate([jnp.where(sel_head == hh, src_rows, 0.0) for hh in (h0, h1)], axis=1)
            seg = jnp.dot(acs_sp, sel.astype(BF16), preferred_element_type=F32)
            m_pair = (cbm2 * jnp.exp(seg + neg_mask2)).astype(BF16)
            x_pair = [jnp.where(lo_half, x_bf, jnp.zeros_like(x_bf)), jnp.where(lo_half, jnp.zeros_like(x_bf), x_bf)]
            y_slab = jnp.dot(m_pair, jnp.concatenate(x_pair, axis=0), preferred_element_type=F32)
            skip = dsk_ref[0, d, :, cols]
            y_ref[0, :, cols] = y_slab + y_off[:, cols] * scale_off[:, cols] + skip * x_slab
            xw_slabs.append((x_slab * w_end[:, cols]).astype(BF16))
        xw = jnp.concatenate(xw_slabs, axis=-1)
        states = lax.dot_general(xw, bmat, (((0,), (0,)), ((), ())), preferred_element_type=F32)
        for hh in range(SSM_HPG):
            rows = slice(hh * P, (hh + 1) * P)
            hl_ref[0, d, rows, :] = h_prev[rows, :] * e_tot_t[hh:hh + 1, :] + states[rows, :]


def _ssd(xbc, dt, a, d_skip, h0):
    bsz, L, _ = xbc.shape
    nc = L // SSM_CHUNK
    G, HPG, Q, N = SSM_GROUPS, SSM_HPG, SSM_CHUNK, SSM_STATE
    gw = SSM_GROUP_W
    W = SSM_SPLIT_W
    dt_g = jnp.transpose(dt.reshape(bsz, L, 2, G, HPG), (2, 0, 3, 1, 4))
    dtt_g = jnp.transpose(dt_g, (0, 1, 2, 4, 3))
    a_g = jnp.transpose(a.reshape(2, G, 1, HPG), (1, 0, 2, 3))
    at_g = jnp.transpose(a_g, (0, 1, 3, 2))
    widen = lambda t: jnp.concatenate([t, t, t, jnp.zeros(t.shape[:-1] + (W - 3 * HPG,), t.dtype)], -1)
    dt_g, a_g = widen(dt_g), widen(a_g)
    dsk = jnp.repeat(d_skip.reshape(2, G, HPG), SSM_HEAD_DIM, axis=-1)
    dsk = jnp.transpose(dsk, (1, 0, 2)).reshape(G, 2, 1, gw)
    nb_x = SSM_D_INNER // N
    fwd = lambda c: c
    bwd = lambda c: nc - 1 - c
    x_spec = lambda f: pl.BlockSpec((1, Q, gw), lambda b, g, c: (b, f(c), g))
    b_spec = lambda f: pl.BlockSpec((1, Q, N), lambda b, g, c: (b, f(c), nb_x + g))
    c_spec = lambda f: pl.BlockSpec((1, Q, N), lambda b, g, c: (b, f(c), nb_x + G + g))
    dt_spec = lambda d, f: pl.BlockSpec((1, 1, 1, Q, W), lambda b, g, c: (d, b, g, f(c), 0))
    dtt_spec = lambda d, f: pl.BlockSpec((1, 1, 1, HPG, Q), lambda b, g, c: (d, b, g, 0, f(c)))
    y_spec = lambda f: pl.BlockSpec((1, Q, gw), lambda b, g, c: (b, f(c), g))
    h_spec = pl.BlockSpec((1, 2, gw, N), lambda b, g, c: (b, 0, g, 0))
    return pl.pallas_call(
        _ssd_kernel,
        grid=(bsz, G, nc),
        in_specs=[x_spec(fwd), x_spec(bwd), b_spec(fwd), b_spec(bwd), c_spec(fwd), c_spec(bwd),
                  dt_spec(0, fwd), dt_spec(1, bwd), dtt_spec(0, fwd), dtt_spec(1, bwd),
                  pl.BlockSpec((1, 2, 1, W), lambda b, g, c: (g, 0, 0, 0)),
                  pl.BlockSpec((1, 2, HPG, 1), lambda b, g, c: (g, 0, 0, 0)),
                  pl.BlockSpec((1, 2, 1, gw), lambda b, g, c: (g, 0, 0, 0)),
                  h_spec],
        out_specs=[y_spec(fwd), y_spec(bwd), h_spec],
        out_shape=[jax.ShapeDtypeStruct((bsz, L, SSM_D_INNER), F32),
                   jax.ShapeDtypeStruct((bsz, L, SSM_D_INNER), F32),
                   jax.ShapeDtypeStruct(h0.shape, F32)],
        compiler_params=_params("parallel", "parallel", "arbitrary"),
    )(xbc, xbc, xbc, xbc, xbc, xbc, dt_g, dt_g, dtt_g, dtt_g, a_g, at_g, dsk, h0)


def _rms_norm(x, g, eps=1e-6):
    return (x * lax.rsqrt(jnp.mean(jnp.square(x), -1, keepdims=True) + eps)) * g


def _rope_2d(x):
    L = x.shape[1]
    half = x.shape[-1] // 2
    quarter = half // 2
    t = jnp.arange(L)
    inv = ROPE_THETA ** (-jnp.arange(quarter, dtype=F32) / quarter)
    ang_r = (t // GRID_W).astype(F32)[:, None] * inv
    ang_c = (t % GRID_W).astype(F32)[:, None] * inv
    bshape = (L,) + (1,) * (x.ndim - 3) + (quarter,)

    def rot(v, ang):
        cos = jnp.cos(ang).reshape(bshape).astype(v.dtype)
        sin = jnp.sin(ang).reshape(bshape).astype(v.dtype)
        v1, v2 = v[..., :quarter], v[..., quarter:]
        return jnp.concatenate([v1 * cos - v2 * sin, v2 * cos + v1 * sin], -1)

    return jnp.concatenate([rot(x[..., :half], ang_r), rot(x[..., half:], ang_c)], -1)


def _mla_pack_q(q_nope, q_pe):
    b, L, h, _ = q_nope.shape
    pad = jnp.zeros((b, L, h, MLA_QK_PAD - MLA_NOPE - MLA_ROPE_DIM), BF16)
    return jnp.concatenate([q_nope.astype(BF16), q_pe.astype(BF16), pad], -1).reshape(b, L, h * MLA_QK_PAD)


def _mla_pack_k(k_nope, k_pe):
    b, L, h, _ = k_nope.shape
    kpe = jnp.broadcast_to(k_pe.astype(BF16)[:, :, None, :], (b, L, h, MLA_ROPE_DIM))
    pad = jnp.zeros((b, L, h, MLA_QK_PAD - MLA_NOPE - MLA_ROPE_DIM), BF16)
    return jnp.concatenate([k_nope.astype(BF16), kpe, pad], -1).reshape(b, L, h * MLA_QK_PAD)


def _ssm_mixer(h, bp, lp, bs, ls, h0_s, w_in, conv_w, conv_b, dt_bias, a_log, d_skip, norm_w):
    tp = bp * lp
    w_in = w_in.astype(BF16)
    z = _matmul(h, w_in[:, :SSM_D_INNER])
    xbc = _matmul(h, w_in[:, SSM_D_INNER:SSM_D_INNER + SSM_CONV_CH])
    dt_raw = _matmul(h, w_in[:, SSM_D_INNER + SSM_CONV_CH:])
    dt = jax.nn.softplus(dt_raw.reshape(-1, 2, SSM_HEADS) + dt_bias)
    a = -jnp.exp(a_log)
    acts = []
    st_p = None
    for (t0, bsz, L, h0) in ((0, bp, lp, None), (tp, bs, ls, h0_s)):
        n_tok = bsz * L
        xa = _conv_silu(xbc, t0, bsz, L, conv_w, conv_b)
        if h0 is None:
            h0 = jnp.zeros((bsz, 2, SSM_D_INNER, SSM_STATE), F32)
        else:
            h0 = h0.reshape(bsz, 2, SSM_D_INNER, SSM_STATE)
        y_f, y_b, h_last = _ssd(xa, dt[t0:t0 + n_tok].reshape(bsz, L, 2, SSM_HEADS), a, d_skip, h0)
        if st_p is None:
            st_p = h_last.reshape(bsz, 2, SSM_HEADS, SSM_HEAD_DIM, SSM_STATE)
        acts.append(_gate_norm(y_f.reshape(n_tok, SSM_D_INNER), y_b.reshape(n_tok, SSM_D_INNER), z, t0, norm_w))
    return jnp.concatenate(acts, 0), st_p


def _attn_mixer(h, bp, lp, bs, ls, c_k, c_v, c_ckv, c_kpe, w_in, rpb, q_norm, w_uq, kv_norm, w_ukv):
    tp, ts = bp * lp, bs * ls
    w_in = w_in.astype(BF16)
    o_qkv, o_cq, o_ckv = 3 * NA_QKV, 3 * NA_QKV + MLA_Q_LORA, 3 * NA_QKV + MLA_Q_LORA + MLA_KV_LORA
    qkv = _matmul(h, w_in[:, :o_qkv], out_dtype=BF16)
    kv_p32 = _matmul(h[:tp], w_in[:, NA_QKV:o_qkv])
    n_lat = w_in.shape[1] - o_qkv
    lat = _matmul(h, jnp.pad(w_in[:, o_qkv:], ((0, 0), (0, MLA_LATENT_PAD - n_lat))))
    cq, ckv, kpe = lat[:, :MLA_Q_LORA], lat[:, MLA_Q_LORA:o_ckv - o_qkv], lat[:, o_ckv - o_qkv:n_lat]
    ckv = _rms_norm(ckv, kv_norm, MLA_NORM_EPS)
    q_all = _matmul(_rms_norm(cq, q_norm, MLA_NORM_EPS).astype(BF16), w_uq)
    lc = c_ckv.shape[1]
    kv_all = _matmul(jnp.concatenate([ckv, c_ckv.reshape(-1, MLA_KV_LORA)], 0).astype(BF16), w_ukv)

    hq = MLA_NOPE + MLA_ROPE_DIM
    hkv = MLA_NOPE + MLA_V
    mla_scale = hq ** -0.5
    mla = functools.partial(_mha, heads=MLA_HEADS, dk=MLA_QK_PAD, dv=MLA_V, scale=mla_scale)

    o_na_p = _mha(qkv, qkv, qkv, bp, lp, lp, NA_HEADS, NA_HEAD_DIM, NA_HEAD_DIM, NA_HEAD_DIM ** -0.5,
                  q_col=0, k_col=1, v_col=2)
    q_p = q_all[:tp].reshape(bp, lp, MLA_HEADS, hq)
    kv_p = kv_all[:tp].reshape(bp, lp, MLA_HEADS, hkv)
    o_mla_p = mla(_mla_pack_q(q_p[..., :MLA_NOPE], q_p[..., MLA_NOPE:]).reshape(tp, -1),
                  _mla_pack_k(kv_p[..., :MLA_NOPE], kpe[:tp].reshape(bp, lp, -1)).reshape(tp, -1),
                  kv_p[..., MLA_NOPE:].astype(BF16).reshape(tp, -1), bsz=bp, lq=lp, lk=lp)

    o_na_s = _na_latent(qkv, tp, bs, ls, c_k.astype(BF16).reshape(bs, -1, NA_QKV),
                        c_v.astype(BF16).reshape(bs, -1, NA_QKV), rpb)
    q_s = q_all[tp:].reshape(bs, ls, MLA_HEADS, hq)
    kv_s = jnp.concatenate([kv_all[tp:tp + ts].reshape(bs, ls, MLA_HEADS, hkv),
                            kv_all[tp + ts:].reshape(bs, lc, MLA_HEADS, hkv)], 1)
    kpe_s = jnp.concatenate([_rope_2d(kpe[tp:].reshape(bs, ls, -1)), c_kpe], 1)
    o_mla_s = mla(_mla_pack_q(q_s[..., :MLA_NOPE], _rope_2d(q_s[..., MLA_NOPE:])).reshape(ts, -1),
                  _mla_pack_k(kv_s[..., :MLA_NOPE], kpe_s).reshape(bs * (ls + lc), -1),
                  kv_s[..., MLA_NOPE:].astype(BF16).reshape(bs * (ls + lc), -1), bsz=bs, lq=ls, lk=ls + lc)

    o_cat = jnp.concatenate([jnp.concatenate([o_na_p, o_na_s], 0), jnp.concatenate([o_mla_p, o_mla_s], 0)], -1)
    new = (kv_p32[:, :NA_QKV].reshape(bp, lp, NA_HEADS, NA_HEAD_DIM),
           kv_p32[:, NA_QKV:].reshape(bp, lp, NA_HEADS, NA_HEAD_DIM),
           ckv[:tp].reshape(bp, lp, -1), kpe[:tp].reshape(bp, lp, -1))
    return o_cat, new


def _modulation(cvec, w, b):
    return jnp.split(jax.nn.silu(cvec) @ w + b, 6, axis=-1)


def kernel(x_prompt, x_sample, c, cache_na_k, cache_na_v, cache_mla_ckv, cache_mla_kpe, state_ssm,
           c_ctx, w_ada, b_ada, ln_g, ln_b, attn_w_in, na_rpb, mla_q_norm, mla_w_uq, mla_kv_norm,
           mla_w_ukv, attn_w_out, ssm_w_in, ssm_conv_w, ssm_conv_b, ssm_dt_bias, ssm_a_log, ssm_d,
           ssm_norm, ssm_w_out, moe_w_router, moe_b_router, moe_w_gu, moe_b_gu, moe_w_down, moe_b_down):
    bp, lp, D = x_prompt.shape
    bs, ls, _ = x_sample.shape
    tp = bp * lp
    tiles_p, tiles_per_s = tp // ROW_TILE, ls // ROW_TILE
    new_k, new_v, new_ckv, new_kpe, new_ssm = [], [], [], [], []

    cond = jnp.concatenate([c_ctx[None, :], c], 0)
    mods = [_modulation(cond, w_ada[l], b_ada[l]) for l in range(DEPTH)]
    zero = jnp.zeros_like(mods[0][0])

    x = jnp.concatenate([x_prompt.reshape(tp, D), x_sample.reshape(bs * ls, D)], 0)
    shift1, scale1 = mods[0][0], mods[0][1]
    h = jnp.concatenate(
        [(x_prompt * (1.0 + scale1[0]) + shift1[0]).reshape(tp, D),
         (x_sample * (1.0 + scale1[1:, None, :]) + shift1[1:, None, :]).reshape(bs * ls, D)], 0).astype(BF16)

    for l in range(DEPTH):
        m = mods[l]
        if l % 2 == 0:
            i = l // 2
            act, (k_c, v_c, ckv_c, kpe_c) = _attn_mixer(
                h, bp, lp, bs, ls, cache_na_k[:, i], cache_na_v[:, i], cache_mla_ckv[:, i], cache_mla_kpe[:, i],
                attn_w_in[i], na_rpb[i], mla_q_norm[i], mla_w_uq[i], mla_kv_norm[i], mla_w_ukv[i])
            w_out = attn_w_out[i]
            new_k.append(k_c)
            new_v.append(v_c)
            new_ckv.append(ckv_c)
            new_kpe.append(kpe_c)
        else:
            j = l // 2
            act, st = _ssm_mixer(h, bp, lp, bs, ls, state_ssm[:, j], ssm_w_in[j], ssm_conv_w[j], ssm_conv_b[j],
                                 ssm_dt_bias[j], ssm_a_log[j], ssm_d[j], ssm_norm[j])
            w_out = ssm_w_out[j]
            new_ssm.append(st)
        x1, hm, top_e, gates = _proj_res_ln(act, w_out, x, _mod_table(m[2], m[3], m[4]),
                                            jnp.stack([ln_g[l, 0], ln_b[l, 0]]), moe_w_router[l], moe_b_router[l],
                                            tiles_p, tiles_per_s)
        yg = _moe(hm, top_e, _cast_bf16(moe_w_gu, l), moe_b_gu[l], _cast_bf16(moe_w_down, l), moe_b_down[l])
        nxt = mods[l + 1] if l + 1 < DEPTH else (zero, zero)
        x, h = _combine_ln(yg, gates, x1, _mod_table(m[5], nxt[0], nxt[1]),
                           jnp.stack([ln_g[l, 1], ln_b[l, 1]]), tiles_p, tiles_per_s)
    return (x[:tp].reshape(bp, lp, D), x[tp:].reshape(bs, ls, D), jnp.stack(new_k, 1), jnp.stack(new_v, 1),
            jnp.stack(new_ckv, 1), jnp.stack(new_kpe, 1), jnp.stack(new_ssm, 1))
```

```python
import functools
import math

import jax
import jax.numpy as jnp
import numpy as np
from jax import lax
from jax.experimental import pallas as pl
from jax.experimental.pallas import tpu as pltpu

D_MODEL = 2048
DEPTH = 4
GRID_W = 64

NA_HEADS = 8
NA_HEAD_DIM = 128
NA_WR = 8
NA_WC = 16

MLA_HEADS = 8
MLA_Q_LORA = 768
MLA_KV_LORA = 512
MLA_NOPE = 128
MLA_ROPE_DIM = 64
MLA_V = 128
MLA_QK_PAD = 256
MLA_LATENT_PAD = 1536
ROPE_THETA = 10000.0

NA_QKV = NA_HEADS * NA_HEAD_DIM

SSM_D_INNER = 2 * D_MODEL
SSM_HEAD_DIM = 64
SSM_HEADS = SSM_D_INNER // SSM_HEAD_DIM
SSM_GROUPS = 8
SSM_HPG = SSM_HEADS // SSM_GROUPS
SSM_GROUP_W = SSM_HPG * SSM_HEAD_DIM
SSM_STATE = 128
SSM_SPLIT_W = 6 * SSM_HPG
SSM_CONV_W = 5
SSM_CHUNK = 128
SSM_CONV_CH = SSM_D_INNER + 2 * SSM_GROUPS * SSM_STATE

N_EXPERTS = 32
TOP_K = 4
D_FF = D_MODEL // 2
SWIGLU_LIMIT = 7.0
SWIGLU_ALPHA = 1.702

DEEPNORM_ALPHA = (2.0 * DEPTH) ** 0.25
LN_EPS = 1e-5
MASK_NEG = -1e30
SSM_NORM_EPS = 1e-5
MLA_NORM_EPS = 1e-6

V7X_VMEM_LIMIT_BYTES = 48 * 1024 * 1024
V7X_LANES = 128
V7X_SUBLANES = 8
ROW_TILE = 256
MM_TILE_M = 1024
MM_TILE_N = 1024
MOE_TILE_M = 512
CAST_TILE_K = 512
ATTN_TILE_Q = 256
NA_ROWS_PER_STEP = 2
CONV_TILE_ELEMS = 1024 * 1024
CONV_HALO = 8

BF16 = jnp.bfloat16
F32 = jnp.float32
_NT = (((1,), (1,)), ((), ()))


def _params(*sem):
    return pltpu.CompilerParams(dimension_semantics=sem, vmem_limit_bytes=V7X_VMEM_LIMIT_BYTES)


def _mm_kernel(x_ref, w_ref, o_ref):
    o_ref[...] = jnp.dot(x_ref[...], w_ref[...], preferred_element_type=F32).astype(o_ref.dtype)


def _pick_tile(n, pref):
    for t in (pref, 768, 512, 384, 256, 128):
        if t <= pref and n % t == 0:
            return t
    return n


def _matmul(x, w, out_dtype=F32):
    m, k = x.shape
    n = w.shape[1]
    tm = _pick_tile(m, MM_TILE_M)
    tn = _pick_tile(n, MM_TILE_N)
    return pl.pallas_call(
        _mm_kernel,
        grid=(m // tm, n // tn),
        in_specs=[pl.BlockSpec((tm, k), lambda i, j: (i, 0)),
                  pl.BlockSpec((k, tn), lambda i, j: (0, j))],
        out_specs=pl.BlockSpec((tm, tn), lambda i, j: (i, j)),
        out_shape=jax.ShapeDtypeStruct((m, n), out_dtype),
        compiler_params=_params("parallel", "parallel"),
    )(x.astype(BF16), w.astype(BF16))


def _cast_kernel(x_ref, o_ref):
    o_ref[...] = x_ref[0].astype(o_ref.dtype)


def _cast_bf16(w, l):
    _, e, k, n = w.shape
    tk = _pick_tile(k, CAST_TILE_K)
    return pl.pallas_call(
        _cast_kernel,
        grid=(e, k // tk),
        in_specs=[pl.BlockSpec((1, 1, tk, n), lambda i, j: (l, i, j, 0))],
        out_specs=pl.BlockSpec((1, tk, n), lambda i, j: (i, j, 0)),
        out_shape=jax.ShapeDtypeStruct((e, k, n), BF16),
        compiler_params=_params("parallel", "parallel"),
    )(w)


def _req_of_tile(i, tiles_p, tiles_per_s):
    return jnp.where(i < tiles_p, 0, 1 + (i - tiles_p) // tiles_per_s)


def _mod_table(*vecs):
    r, d = vecs[0].shape
    pad = [jnp.zeros((r, d), F32)] * (V7X_SUBLANES - len(vecs))
    return jnp.stack(list(vecs) + pad, axis=1)


def _ln_mod(u, g, b, shift, scale):
    mu = jnp.mean(u, -1, keepdims=True)
    var = jnp.mean(jnp.square(u - mu), -1, keepdims=True)
    xn = (u - mu) * lax.rsqrt(var + LN_EPS) * g + b
    return xn, xn * (1.0 + scale) + shift


def _proj_res_ln_kernel(a_ref, w_ref, x_ref, mod_ref, ln_ref, wrh_ref, wrl_ref, br_ref,
                        x1_ref, hm_ref, te_ref, gt_ref):
    y = jnp.dot(a_ref[...], w_ref[...], preferred_element_type=F32)
    u = DEEPNORM_ALPHA * x_ref[...] + mod_ref[0, 0:1, :] * y
    x1, hm = _ln_mod(u, ln_ref[0:1, :], ln_ref[1:2, :], mod_ref[0, 1:2, :], mod_ref[0, 2:3, :])
    x1_ref[...] = x1
    hm_hi = hm.astype(BF16)
    hm_ref[...] = hm_hi
    hm_lo = (hm - hm_hi.astype(F32)).astype(BF16)
    logits = (jnp.dot(hm_hi, wrh_ref[...], preferred_element_type=F32)
              + jnp.dot(hm_hi, wrl_ref[...], preferred_element_type=F32)
              + jnp.dot(hm_lo, wrh_ref[...], preferred_element_type=F32) + br_ref[...])
    ne = logits.shape[1]
    expert = lax.broadcasted_iota(jnp.int32, logits.shape, 1).astype(F32)
    slot = lax.broadcasted_iota(jnp.int32, te_ref.shape, 1)
    top_v = jnp.zeros(te_ref.shape, F32)
    top_e = jnp.zeros(te_ref.shape, F32)
    for kk in range(TOP_K):
        best = jnp.max(logits, -1, keepdims=True)
        best_e = jnp.min(jnp.where(logits == best, expert, float(ne)), -1, keepdims=True)
        top_v = jnp.where(slot == kk, best, top_v)
        top_e = jnp.where(slot == kk, best_e, top_e)
        logits = jnp.where(expert == best_e, -jnp.inf, logits)
    p = jnp.exp(top_v - jnp.max(top_v, -1, keepdims=True))
    te_ref[...] = top_e.astype(jnp.int32)
    gt_ref[...] = p / jnp.sum(p, -1, keepdims=True)


def _proj_res_ln(a, w, x, mod, ln, w_router, b_router, tiles_p, tiles_per_s):
    t, k = a.shape
    d = w.shape[1]
    ne = w_router.shape[1]
    tm = ROW_TILE
    req = lambda i: (_req_of_tile(i, tiles_p, tiles_per_s), 0, 0)
    wr_hi = w_router.astype(BF16)
    wr_lo = (w_router - wr_hi.astype(F32)).astype(BF16)
    return pl.pallas_call(
        _proj_res_ln_kernel,
        grid=(t // tm,),
        in_specs=[pl.BlockSpec((tm, k), lambda i: (i, 0)),
                  pl.BlockSpec((k, d), lambda i: (0, 0), pipeline_mode=pl.Buffered(1)),
                  pl.BlockSpec((tm, d), lambda i: (i, 0)),
                  pl.BlockSpec((1, V7X_SUBLANES, d), req),
                  pl.BlockSpec((2, d), lambda i: (0, 0)),
                  pl.BlockSpec((d, ne), lambda i: (0, 0)),
                  pl.BlockSpec((d, ne), lambda i: (0, 0)),
                  pl.BlockSpec((1, ne), lambda i: (0, 0))],
        out_specs=[pl.BlockSpec((tm, d), lambda i: (i, 0)),
                   pl.BlockSpec((tm, d), lambda i: (i, 0)),
                   pl.BlockSpec((tm, TOP_K), lambda i: (i, 0)),
                   pl.BlockSpec((tm, TOP_K), lambda i: (i, 0))],
        out_shape=[jax.ShapeDtypeStruct((t, d), F32), jax.ShapeDtypeStruct((t, d), BF16),
                   jax.ShapeDtypeStruct((t, TOP_K), jnp.int32), jax.ShapeDtypeStruct((t, TOP_K), F32)],
        compiler_params=_params("parallel"),
    )(a, w.astype(BF16), x, mod, ln, wr_hi, wr_lo, b_router.reshape(1, ne))


def _combine_ln_kernel(*refs):
    y_refs = refs[:TOP_K]
    gt_ref, x_ref, mod_ref, ln_ref, x2_ref, h_ref = refs[TOP_K:]
    gt = gt_ref[...]
    ym = gt[:, 0:1] * y_refs[0][...].astype(F32)
    for kk in range(1, TOP_K):
        ym = ym + gt[:, kk:kk + 1] * y_refs[kk][...].astype(F32)
    u = DEEPNORM_ALPHA * x_ref[...] + mod_ref[0, 0:1, :] * ym
    x2, h = _ln_mod(u, ln_ref[0:1, :], ln_ref[1:2, :], mod_ref[0, 1:2, :], mod_ref[0, 2:3, :])
    x2_ref[...] = x2
    h_ref[...] = h.astype(h_ref.dtype)


def _combine_ln(yg, gates, x, mod, ln, tiles_p, tiles_per_s):
    t, d = x.shape
    tm = ROW_TILE
    nt = t // tm
    req = lambda i: (_req_of_tile(i, tiles_p, tiles_per_s), 0, 0)
    y_specs = [pl.BlockSpec((tm, d), functools.partial(lambda i, kk: (kk * nt + i, 0), kk=kk))
               for kk in range(TOP_K)]
    return pl.pallas_call(
        _combine_ln_kernel,
        grid=(nt,),
        in_specs=y_specs + [pl.BlockSpec((tm, TOP_K), lambda i: (i, 0)),
                            pl.BlockSpec((tm, d), lambda i: (i, 0)),
                            pl.BlockSpec((1, V7X_SUBLANES, d), req),
                            pl.BlockSpec((2, d), lambda i: (0, 0))],
        out_specs=[pl.BlockSpec((tm, d), lambda i: (i, 0)),
                   pl.BlockSpec((tm, d), lambda i: (i, 0))],
        out_shape=[jax.ShapeDtypeStruct((t, d), F32), jax.ShapeDtypeStruct((t, d), BF16)],
        compiler_params=_params("parallel"),
    )(*([yg] * TOP_K), gates, x, mod, ln)


def _moe_ffn_kernel(be_ref, nused_ref, x_ref, wgu_ref, bgu_ref, wd_ref, bd_ref, o_ref):
    @pl.when(pl.program_id(0) < nused_ref[0])
    def _():
        x = x_ref[...]
        gu = jnp.dot(x, wgu_ref[0], preferred_element_type=F32) + bgu_ref[0]
        x_glu = jnp.minimum(gu[:, :D_FF], SWIGLU_LIMIT)
        x_lin = jnp.clip(gu[:, D_FF:], -SWIGLU_LIMIT, SWIGLU_LIMIT)
        act = x_glu * jax.nn.sigmoid(SWIGLU_ALPHA * x_glu) * (x_lin + 1.0)
        y = jnp.dot(act.astype(BF16), wd_ref[0], preferred_element_type=F32) + bd_ref[0]
        o_ref[...] = y.astype(o_ref.dtype)

    @pl.when(pl.program_id(0) >= nused_ref[0])
    def _():
        o_ref[...] = jnp.zeros_like(o_ref)


def _moe_ffn(xb, block_e, n_used, w_gu, b_gu, w_down, b_down):
    rows, d = xb.shape
    nb = rows // MOE_TILE_M
    grid_spec = pltpu.PrefetchScalarGridSpec(
        num_scalar_prefetch=2,
        grid=(nb,),
        in_specs=[pl.BlockSpec((MOE_TILE_M, d), lambda i, be, nu: (i, 0)),
                  pl.BlockSpec((1, d, 2 * D_FF), lambda i, be, nu: (be[i], 0, 0)),
                  pl.BlockSpec((1, 1, 2 * D_FF), lambda i, be, nu: (be[i], 0, 0)),
                  pl.BlockSpec((1, D_FF, d), lambda i, be, nu: (be[i], 0, 0)),
                  pl.BlockSpec((1, 1, d), lambda i, be, nu: (be[i], 0, 0))],
        out_specs=pl.BlockSpec((MOE_TILE_M, d), lambda i, be, nu: (i, 0)),
    )
    return pl.pallas_call(
        _moe_ffn_kernel,
        grid_spec=grid_spec,
        out_shape=jax.ShapeDtypeStruct((rows, d), BF16),
        compiler_params=_params("arbitrary"),
    )(block_e, n_used, xb, w_gu, b_gu.reshape(N_EXPERTS, 1, 2 * D_FF), w_down,
      b_down.reshape(N_EXPERTS, 1, d))


def _moe(hm, top_e, w_gu, b_gu, w_down, b_down):
    T, d = hm.shape
    tm = MOE_TILE_M
    n = T * TOP_K
    flat_e = top_e.reshape(-1)
    iota_n = jnp.arange(n, dtype=jnp.int32)
    sorted_e, order = lax.sort((flat_e, iota_n), num_keys=1, is_stable=True)
    bounds = jnp.sum(flat_e[:, None] < jnp.arange(N_EXPERTS + 1, dtype=jnp.int32)[None, :], axis=0, dtype=jnp.int32)
    start, counts = bounds[:-1], bounds[1:] - bounds[:-1]
    padded = (counts + tm - 1) // tm * tm
    pad_end = jnp.cumsum(padded)
    pad_start = pad_end - padded
    pos_sorted = pad_start[sorted_e] + iota_n - start[sorted_e]
    _, pos = lax.sort((order, pos_sorted), num_keys=1)
    nb = n // tm + N_EXPERTS
    n_used = (pad_end[-1] // tm).astype(jnp.int32)
    blk = jnp.arange(nb, dtype=jnp.int32)
    block_e = jnp.minimum(jnp.sum(pad_end[None, :] <= (blk * tm)[:, None], axis=1, dtype=jnp.int32), N_EXPERTS - 1)
    last_e = block_e[jnp.maximum(n_used - 1, 0)]
    block_e = jnp.where(blk < n_used, block_e, last_e)
    r = jnp.arange(nb * tm, dtype=jnp.int32)
    row_e = jnp.repeat(block_e, tm)
    off = r - pad_start[row_e]
    valid = (off < counts[row_e]) & (r < pad_end[-1])
    src = jnp.clip(start[row_e] + off, 0, n - 1)
    row_tok = jnp.where(valid, order[src] // TOP_K, 0)
    xb = hm[row_tok]
    yb = _moe_ffn(xb, block_e, n_used.reshape(1), w_gu, b_gu, w_down, b_down)
    pos_kt = jnp.transpose(pos.reshape(T, TOP_K)).reshape(-1)
    return yb[pos_kt]


def _softmax_pv(s_list, v_list):
    m = functools.reduce(jnp.maximum, [jnp.max(s, -1, keepdims=True) for s in s_list])
    p_list = [jnp.exp(s - m) for s in s_list]
    denom = functools.reduce(jnp.add, [jnp.sum(p, -1, keepdims=True) for p in p_list])
    o = functools.reduce(jnp.add, [jnp.dot(p.astype(BF16), v, preferred_element_type=F32)
                                   for p, v in zip(p_list, v_list)])
    return o / denom


def _mha_kernel(q_ref, k_ref, v_ref, o_ref, *, heads, dk, dv, scale):
    for h in range(heads):
        q = q_ref[:, h * dk:(h + 1) * dk]
        k = k_ref[:, h * dk:(h + 1) * dk]
        s = lax.dot_general(q, k, _NT, preferred_element_type=F32) * scale
        o = _softmax_pv([s], [v_ref[:, h * dv:(h + 1) * dv]])
        o_ref[:, h * dv:(h + 1) * dv] = o.astype(o_ref.dtype)


def _mha(q, k, v, bsz, lq, lk, heads, dk, dv, scale, q_col=0, k_col=0, v_col=0):
    tq = min(ATTN_TILE_Q, lq)
    nq = lq // tq
    return pl.pallas_call(
        functools.partial(_mha_kernel, heads=heads, dk=dk, dv=dv, scale=scale),
        grid=(bsz, nq),
        in_specs=[pl.BlockSpec((tq, heads * dk), lambda i, j: (i * nq + j, q_col)),
                  pl.BlockSpec((lk, heads * dk), lambda i, j: (i, k_col)),
                  pl.BlockSpec((lk, heads * dv), lambda i, j: (i, v_col))],
        out_specs=pl.BlockSpec((tq, heads * dv), lambda i, j: (i * nq + j, 0)),
        out_shape=jax.ShapeDtypeStruct((bsz * lq, heads * dv), BF16),
        compiler_params=_params("parallel", "arbitrary"),
    )(q, k, v)


def _na_kernel(q_ref, k_ref, v_ref, kc_ref, vc_ref, bias_ref, o_ref, *, heads, d, scale, grid_rows, win):
    rows_per_step = bias_ref.shape[0]
    for rr in range(rows_per_step):
        r = pl.program_id(1) * rows_per_step + rr
        rs = jnp.clip(r - NA_WR // 2, 0, grid_rows - win)
        start = pl.multiple_of(rs * GRID_W, GRID_W)
        for h in range(heads):
            hs = slice(h * d, (h + 1) * d)
            q = q_ref[rr * GRID_W:(rr + 1) * GRID_W, hs]
            k_loc = k_ref[pl.ds(start, win * GRID_W), hs]
            v_loc = v_ref[pl.ds(start, win * GRID_W), hs]
            s_loc = lax.dot_general(q, k_loc, _NT, preferred_element_type=F32) * scale + bias_ref[rr, h]
            s_ctx = lax.dot_general(q, kc_ref[0, :, hs], _NT, preferred_element_type=F32) * scale
            o = _softmax_pv([s_loc, s_ctx], [v_loc, vc_ref[0, :, hs]])
            o_ref[rr * GRID_W:(rr + 1) * GRID_W, hs] = o.astype(o_ref.dtype)


def _na_bias(rpb, grid_rows, win):
    qc = np.arange(GRID_W)
    kc = np.arange(GRID_W)
    qcs = np.clip(qc - NA_WC // 2, 0, GRID_W - NA_WC)
    col_ok = (kc[None, :] >= qcs[:, None]) & (kc[None, :] < qcs[:, None] + NA_WC)
    col_idx = np.clip(kc[None, :] - qc[:, None] + NA_WC - 1, 0, 2 * NA_WC - 2)
    r = np.arange(grid_rows)
    rs = np.clip(r - NA_WR // 2, 0, grid_rows - win)
    row_idx = rs[:, None] + np.arange(win)[None, :] - r[:, None] + NA_WR - 1
    bias = rpb[:, row_idx][:, :, :, col_idx]
    bias = jnp.where(jnp.asarray(col_ok)[None, None, None], bias.astype(F32), MASK_NEG)
    bias = jnp.transpose(bias, (1, 0, 3, 2, 4))
    return bias.reshape(grid_rows, rpb.shape[0], GRID_W, win * GRID_W)


def _na_latent(qkv, row0, bsz, L, k_ctx, v_ctx, rpb):
    heads = rpb.shape[0]
    hd = qkv.shape[1] // 3
    d = hd // heads
    lc = k_ctx.shape[1]
    grid_rows = L // GRID_W
    win = min(NA_WR, grid_rows)
    rps = NA_ROWS_PER_STEP
    tq = rps * GRID_W
    nq = L // tq
    bias = _na_bias(rpb, grid_rows, win)
    return pl.pallas_call(
        functools.partial(_na_kernel, heads=heads, d=d, scale=d ** -0.5, grid_rows=grid_rows, win=win),
        grid=(bsz, nq),
        in_specs=[pl.BlockSpec((tq, hd), lambda i, j: (row0 // tq + i * nq + j, 0)),
                  pl.BlockSpec((L, hd), lambda i, j: (row0 // L + i, 1)),
                  pl.BlockSpec((L, hd), lambda i, j: (row0 // L + i, 2)),
                  pl.BlockSpec((1, lc, hd), lambda i, j: (i, 0, 0)),
                  pl.BlockSpec((1, lc, hd), lambda i, j: (i, 0, 0)),
                  pl.BlockSpec((rps, heads, GRID_W, win * GRID_W), lambda i, j: (j, 0, 0, 0))],
        out_specs=pl.BlockSpec((tq, hd), lambda i, j: (i * nq + j, 0)),
        out_shape=jax.ShapeDtypeStruct((bsz * L, hd), BF16),
        compiler_params=_params("parallel", "arbitrary"),
    )(qkv, qkv, qkv, k_ctx, v_ctx, bias)


def _conv_silu_kernel(x_ref, w_ref, b_ref, o_ref, pad_ref):
    L = x_ref.shape[0]
    zeros = jnp.zeros((CONV_HALO, pad_ref.shape[1]), F32)
    pad_ref[0:CONV_HALO, :] = zeros
    pad_ref[CONV_HALO + L:, :] = zeros
    pad_ref[CONV_HALO:CONV_HALO + L, :] = x_ref[...]
    half = SSM_CONV_W // 2
    acc = b_ref[...] + w_ref[0:1, :] * pad_ref[CONV_HALO - half:CONV_HALO - half + L, :]
    for t in range(1, SSM_CONV_W):
        off = CONV_HALO - half + t
        acc = acc + w_ref[t:t + 1, :] * pad_ref[off:off + L, :]
    o_ref[0] = acc * jax.nn.sigmoid(acc)


def _conv_silu(x, row0, bsz, L, w, b):
    C = x.shape[1]
    tc = max(t for t in (C // 2, C // 3, C // 6, C // 12) if t % V7X_LANES == 0 and L * t <= CONV_TILE_ELEMS)
    return pl.pallas_call(
        _conv_silu_kernel,
        grid=(bsz, C // tc),
        in_specs=[pl.BlockSpec((L, tc), lambda i, j: (row0 // L + i, j)),
                  pl.BlockSpec((SSM_CONV_W, tc), lambda i, j: (0, j)),
                  pl.BlockSpec((1, tc), lambda i, j: (0, j))],
        out_specs=pl.BlockSpec((1, L, tc), lambda i, j: (i, 0, j)),
        out_shape=jax.ShapeDtypeStruct((bsz, L, C), F32),
        scratch_shapes=[pltpu.VMEM((L + 2 * CONV_HALO, tc), F32)],
        compiler_params=_params("parallel", "parallel"),
    )(x, w, b.reshape(1, C))


def _gate_norm_kernel(yf_ref, yb_ref, z_ref, nw_ref, o_ref):
    z = z_ref[...]
    y = (yf_ref[...] + yb_ref[...]) * (z * jax.nn.sigmoid(z))
    for g in range(SSM_GROUPS):
        cols = slice(g * SSM_GROUP_W, (g + 1) * SSM_GROUP_W)
        seg = y[:, cols]
        ms = jnp.mean(jnp.square(seg), -1, keepdims=True)
        o_ref[:, cols] = (seg * lax.rsqrt(ms + SSM_NORM_EPS) * nw_ref[:, cols]).astype(o_ref.dtype)


def _gate_norm(y_f, y_b, z, row0, norm_w):
    n, c = y_f.shape
    tm = ROW_TILE
    return pl.pallas_call(
        _gate_norm_kernel,
        grid=(n // tm,),
        in_specs=[pl.BlockSpec((tm, c), lambda i: (i, 0)),
                  pl.BlockSpec((tm, c), lambda i: (i, 0)),
                  pl.BlockSpec((tm, c), lambda i: (row0 // tm + i, 0)),
                  pl.BlockSpec((1, c), lambda i: (0, 0))],
        out_specs=pl.BlockSpec((tm, c), lambda i: (i, 0)),
        out_shape=jax.ShapeDtypeStruct((n, c), BF16),
        compiler_params=_params("parallel"),
    )(y_f, y_b, z, norm_w.reshape(1, c))


def _ssd_kernel(xf_ref, xb_ref, bf_ref, bb_ref, cf_ref, cb_ref, dtf_ref, dtb_ref, dttf_ref, dttb_ref,
                a_ref, at_ref, dsk_ref, h0_ref, yf_ref, yb_ref, hl_ref):
    c = pl.program_id(2)
    Q = SSM_CHUNK
    P = SSM_HEAD_DIM

    @pl.when(c == 0)
    def _():
        hl_ref[...] = h0_ref[...]

    row = lax.broadcasted_iota(jnp.int32, (Q, Q), 0)
    col = lax.broadcasted_iota(jnp.int32, (Q, Q), 1)
    lower = (col <= row)
    upper = (col >= row)
    lower_f = lower.astype(F32)
    upper_f = upper.astype(F32)
    lane = lax.broadcasted_iota(jnp.int32, (Q, V7X_LANES), 1)
    lo_half = lane < P
    hi = lax.Precision.HIGHEST
    HPG, W = SSM_HPG, SSM_SPLIT_W
    hpg_bits, p_bits = HPG.bit_length() - 1, P.bit_length() - 1
    grp = lax.broadcasted_iota(jnp.int32, (Q, W), 1) >> hpg_bits
    sel_head = lax.broadcasted_iota(jnp.int32, (W, Q), 0) & (HPG - 1)
    exp_row = lax.broadcasted_iota(jnp.int32, (W, SSM_GROUP_W), 0)
    exp_col = lax.broadcasted_iota(jnp.int32, (W, SSM_GROUP_W), 1)
    expand = jnp.where((exp_row < 3 * HPG) & ((exp_row & (HPG - 1)) == (exp_col >> p_bits)),
                       1.0, 0.0).astype(BF16)

    def pieces(v):
        p1 = v.astype(BF16).astype(F32)
        r1 = v - p1
        p2 = r1.astype(BF16).astype(F32)
        return p1, p2, r1 - p2

    def split3(v, fill):
        p1, p2, p3 = pieces(v)
        return jnp.where(grp == 0, p1, jnp.where(grp == 1, p2, jnp.where(grp == 2, p3, fill))).astype(BF16)

    dirs = ((0, xf_ref, bf_ref, cf_ref, dtf_ref, dttf_ref, yf_ref, lower, lower_f, upper_f, Q - 1),
            (1, xb_ref, bb_ref, cb_ref, dtb_ref, dttb_ref, yb_ref, upper, upper_f, lower_f, 0))
    for (d, x_ref, b_ref, c_ref, dt_ref, dtt_ref, y_ref, valid, cum_mat, cum_mat_t, last) in dirs:
        dt = dt_ref[0, 0, 0]
        dtt = dtt_ref[0, 0, 0]
        a_row = a_ref[0, d]
        a_col = at_ref[0, d]
        acs = jnp.dot(cum_mat, dt * a_row, preferred_element_type=F32, precision=hi)
        acst = jnp.dot(dtt * a_col, cum_mat_t, preferred_element_type=F32, precision=hi)
        src_term = jnp.minimum(acst - jnp.log(dtt), -MASK_NEG)
        neg_mask = jnp.where(valid, 0.0, MASK_NEG)
        total = acs[last:last + 1, :]
        acs_sp = split3(acs, 1.0)
        scale_off = jnp.dot(split3(jnp.exp(acs), 0.0), expand, preferred_element_type=F32)
        w_end = jnp.dot(split3(jnp.exp(total - acs) * dt, 0.0), expand, preferred_element_type=F32)
        src_rows = jnp.concatenate([jnp.ones((3 * HPG, Q), F32)] + [-p for p in pieces(src_term)], axis=0)
        e_tot_t = jnp.exp(acst[:, last:last + 1])
        bmat = b_ref[0].astype(BF16)
        cmat = c_ref[0].astype(BF16)
        cbm = lax.dot_general(cmat, bmat, _NT, preferred_element_type=F32)
        cbm2 = jnp.concatenate([cbm, cbm], axis=1)
        neg_mask2 = jnp.concatenate([neg_mask, neg_mask], axis=1)
        h_prev = hl_ref[0, d]
        y_off = lax.dot_general(cmat, h_prev.astype(BF16), _NT, preferred_element_type=F32)
        xw_slabs = []
        for s in range(SSM_HPG // 2):
            h0, h1 = 2 * s, 2 * s + 1
            cols = slice(s * V7X_LANES, (s + 1) * V7X_LANES)
            x_slab = x_ref[0, :, cols]
            x_bf = x_slab.astype(BF16)
            sel = jnp.concatenate([jnp.where(sel_head == hh, src_rows, 0.0) for hh in (h0, h1)], axis=1)
            seg = jnp.dot(acs_sp, sel.astype(BF16), preferred_element_type=F32)
            m_pair = (cbm2 * jnp.exp(seg + neg_mask2)).astype(BF16)
            x_pair = [jnp.where(lo_half, x_bf, jnp.zeros_like(x_bf)), jnp.where(lo_half, jnp.zeros_like(x_bf), x_bf)]
            y_slab = jnp.dot(m_pair, jnp.concatenate(x_pair, axis=0), preferred_element_type=F32)
            skip = dsk_ref[0, d, :, cols]
            y_ref[0, :, cols] = y_slab + y_off[:, cols] * scale_off[:, cols] + skip * x_slab
            xw_slabs.append((x_slab * w_end[:, cols]).astype(BF16))
        xw = jnp.concatenate(xw_slabs, axis=-1)
        states = lax.dot_general(xw, bmat, (((0,), (0,)), ((), ())), preferred_element_type=F32)
        for hh in range(SSM_HPG):
            rows = slice(hh * P, (hh + 1) * P)
            hl_ref[0, d, rows, :] = h_prev[rows, :] * e_tot_t[hh:hh + 1, :] + states[rows, :]


def _ssd(xbc, dt, a, d_skip, h0):
    bsz, L, _ = xbc.shape
    nc = L // SSM_CHUNK
    G, HPG, Q, N = SSM_GROUPS, SSM_HPG, SSM_CHUNK, SSM_STATE
    gw = SSM_GROUP_W
    W = SSM_SPLIT_W
    dt_g = jnp.transpose(dt.reshape(bsz, L, 2, G, HPG), (2, 0, 3, 1, 4))
    dtt_g = jnp.transpose(dt_g, (0, 1, 2, 4, 3))
    a_g = jnp.transpose(a.reshape(2, G, 1, HPG), (1, 0, 2, 3))
    at_g = jnp.transpose(a_g, (0, 1, 3, 2))
    widen = lambda t: jnp.concatenate([t, t, t, jnp.zeros(t.shape[:-1] + (W - 3 * HPG,), t.dtype)], -1)
    dt_g, a_g = widen(dt_g), widen(a_g)
    dsk = jnp.repeat(d_skip.reshape(2, G, HPG), SSM_HEAD_DIM, axis=-1)
    dsk = jnp.transpose(dsk, (1, 0, 2)).reshape(G, 2, 1, gw)
    nb_x = SSM_D_INNER // N
    fwd = lambda c: c
    bwd = lambda c: nc - 1 - c
    x_spec = lambda f: pl.BlockSpec((1, Q, gw), lambda b, g, c: (b, f(c), g))
    b_spec = lambda f: pl.BlockSpec((1, Q, N), lambda b, g, c: (b, f(c), nb_x + g))
    c_spec = lambda f: pl.BlockSpec((1, Q, N), lambda b, g, c: (b, f(c), nb_x + G + g))
    dt_spec = lambda d, f: pl.BlockSpec((1, 1, 1, Q, W), lambda b, g, c: (d, b, g, f(c), 0))
    dtt_spec = lambda d, f: pl.BlockSpec((1, 1, 1, HPG, Q), lambda b, g, c: (d, b, g, 0, f(c)))
    y_spec = lambda f: pl.BlockSpec((1, Q, gw), lambda b, g, c: (b, f(c), g))
    h_spec = pl.BlockSpec((1, 2, gw, N), lambda b, g, c: (b, 0, g, 0))
    return pl.pallas_call(
        _ssd_kernel,
        grid=(bsz, G, nc),
        in_specs=[x_spec(fwd), x_spec(bwd), b_spec(fwd), b_spec(bwd), c_spec(fwd), c_spec(bwd),
                  dt_spec(0, fwd), dt_spec(1, bwd), dtt_spec(0, fwd), dtt_spec(1, bwd),
                  pl.BlockSpec((1, 2, 1, W), lambda b, g, c: (g, 0, 0, 0)),
                  pl.BlockSpec((1, 2, HPG, 1), lambda b, g, c: (g, 0, 0, 0)),
                  pl.BlockSpec((1, 2, 1, gw), lambda b, g, c: (g, 0, 0, 0)),
                  h_spec],
        out_specs=[y_spec(fwd), y_spec(bwd), h_spec],
        out_shape=[jax.ShapeDtypeStruct((bsz, L, SSM_D_INNER), F32),
                   jax.ShapeDtypeStruct((bsz, L, SSM_D_INNER), F32),
                   jax.ShapeDtypeStruct(h0.shape, F32)],
        compiler_params=_params("parallel", "parallel", "arbitrary"),
    )(xbc, xbc, xbc, xbc, xbc, xbc, dt_g, dt_g, dtt_g, dtt_g, a_g, at_g, dsk, h0)


def _rms_norm(x, g, eps=1e-6):
    return (x * lax.rsqrt(jnp.mean(jnp.square(x), -1, keepdims=True) + eps)) * g


def _rope_2d(x):
    L = x.shape[1]
    half = x.shape[-1] // 2
    quarter = half // 2
    t = jnp.arange(L)
    inv = ROPE_THETA ** (-jnp.arange(quarter, dtype=F32) / quarter)
    ang_r = (t // GRID_W).astype(F32)[:, None] * inv
    ang_c = (t % GRID_W).astype(F32)[:, None] * inv
    bshape = (L,) + (1,) * (x.ndim - 3) + (quarter,)

    def rot(v, ang):
        cos = jnp.cos(ang).reshape(bshape).astype(v.dtype)
        sin = jnp.sin(ang).reshape(bshape).astype(v.dtype)
        v1, v2 = v[..., :quarter], v[..., quarter:]
        return jnp.concatenate([v1 * cos - v2 * sin, v2 * cos + v1 * sin], -1)

    return jnp.concatenate([rot(x[..., :half], ang_r), rot(x[..., half:], ang_c)], -1)


def _mla_pack_q(q_nope, q_pe):
    b, L, h, _ = q_nope.shape
    pad = jnp.zeros((b, L, h, MLA_QK_PAD - MLA_NOPE - MLA_ROPE_DIM), BF16)
    return jnp.concatenate([q_nope.astype(BF16), q_pe.astype(BF16), pad], -1).reshape(b, L, h * MLA_QK_PAD)


def _mla_pack_k(k_nope, k_pe):
    b, L, h, _ = k_nope.shape
    kpe = jnp.broadcast_to(k_pe.astype(BF16)[:, :, None, :], (b, L, h, MLA_ROPE_DIM))
    pad = jnp.zeros((b, L, h, MLA_QK_PAD - MLA_NOPE - MLA_ROPE_DIM), BF16)
    return jnp.concatenate([k_nope.astype(BF16), kpe, pad], -1).reshape(b, L, h * MLA_QK_PAD)


def _ssm_mixer(h, bp, lp, bs, ls, h0_s, w_in, conv_w, conv_b, dt_bias, a_log, d_skip, norm_w):
    tp = bp * lp
    w_in = w_in.astype(BF16)
    z = _matmul(h, w_in[:, :SSM_D_INNER])
    xbc = _matmul(h, w_in[:, SSM_D_INNER:SSM_D_INNER + SSM_CONV_CH])
    dt_raw = _matmul(h, w_in[:, SSM_D_INNER + SSM_CONV_CH:])
    dt = jax.nn.softplus(dt_raw.reshape(-1, 2, SSM_HEADS) + dt_bias)
    a = -jnp.exp(a_log)
    acts = []
    st_p = None
    for (t0, bsz, L, h0) in ((0, bp, lp, None), (tp, bs, ls, h0_s)):
        n_tok = bsz * L
        xa = _conv_silu(xbc, t0, bsz, L, conv_w, conv_b)
        if h0 is None:
            h0 = jnp.zeros((bsz, 2, SSM_D_INNER, SSM_STATE), F32)
        else:
            h0 = h0.reshape(bsz, 2, SSM_D_INNER, SSM_STATE)
        y_f, y_b, h_last = _ssd(xa, dt[t0:t0 + n_tok].reshape(bsz, L, 2, SSM_HEADS), a, d_skip, h0)
        if st_p is None:
            st_p = h_last.reshape(bsz, 2, SSM_HEADS, SSM_HEAD_DIM, SSM_STATE)
        acts.append(_gate_norm(y_f.reshape(n_tok, SSM_D_INNER), y_b.reshape(n_tok, SSM_D_INNER), z, t0, norm_w))
    return jnp.concatenate(acts, 0), st_p


def _attn_mixer(h, bp, lp, bs, ls, c_k, c_v, c_ckv, c_kpe, w_in, rpb, q_norm, w_uq, kv_norm, w_ukv):
    tp, ts = bp * lp, bs * ls
    w_in = w_in.astype(BF16)
    o_qkv, o_cq, o_ckv = 3 * NA_QKV, 3 * NA_QKV + MLA_Q_LORA, 3 * NA_QKV + MLA_Q_LORA + MLA_KV_LORA
    qkv = _matmul(h, w_in[:, :o_qkv], out_dtype=BF16)
    kv_p32 = _matmul(h[:tp], w_in[:, NA_QKV:o_qkv])
    n_lat = w_in.shape[1] - o_qkv
    lat = _matmul(h, jnp.pad(w_in[:, o_qkv:], ((0, 0), (0, MLA_LATENT_PAD - n_lat))))
    cq, ckv, kpe = lat[:, :MLA_Q_LORA], lat[:, MLA_Q_LORA:o_ckv - o_qkv], lat[:, o_ckv - o_qkv:n_lat]
    ckv = _rms_norm(ckv, kv_norm, MLA_NORM_EPS)
    q_all = _matmul(_rms_norm(cq, q_norm, MLA_NORM_EPS).astype(BF16), w_uq)
    lc = c_ckv.shape[1]
    kv_all = _matmul(jnp.concatenate([ckv, c_ckv.reshape(-1, MLA_KV_LORA)], 0).astype(BF16), w_ukv)

    hq = MLA_NOPE + MLA_ROPE_DIM
    hkv = MLA_NOPE + MLA_V
    mla_scale = hq ** -0.5
    mla = functools.partial(_mha, heads=MLA_HEADS, dk=MLA_QK_PAD, dv=MLA_V, scale=mla_scale)

    o_na_p = _mha(qkv, qkv, qkv, bp, lp, lp, NA_HEADS, NA_HEAD_DIM, NA_HEAD_DIM, NA_HEAD_DIM ** -0.5,
                  q_col=0, k_col=1, v_col=2)
    q_p = q_all[:tp].reshape(bp, lp, MLA_HEADS, hq)
    kv_p = kv_all[:tp].reshape(bp, lp, MLA_HEADS, hkv)
    o_mla_p = mla(_mla_pack_q(q_p[..., :MLA_NOPE], q_p[..., MLA_NOPE:]).reshape(tp, -1),
                  _mla_pack_k(kv_p[..., :MLA_NOPE], kpe[:tp].reshape(bp, lp, -1)).reshape(tp, -1),
                  kv_p[..., MLA_NOPE:].astype(BF16).reshape(tp, -1), bsz=bp, lq=lp, lk=lp)

    o_na_s = _na_latent(qkv, tp, bs, ls, c_k.astype(BF16).reshape(bs, -1, NA_QKV),
                        c_v.astype(BF16).reshape(bs, -1, NA_QKV), rpb)
    q_s = q_all[tp:].reshape(bs, ls, MLA_HEADS, hq)
    kv_s = jnp.concatenate([kv_all[tp:tp + ts].reshape(bs, ls, MLA_HEADS, hkv),
                            kv_all[tp + ts:].reshape(bs, lc, MLA_HEADS, hkv)], 1)
    kpe_s = jnp.concatenate([_rope_2d(kpe[tp:].reshape(bs, ls, -1)), c_kpe], 1)
    o_mla_s = mla(_mla_pack_q(q_s[..., :MLA_NOPE], _rope_2d(q_s[..., MLA_NOPE:])).reshape(ts, -1),
                  _mla_pack_k(kv_s[..., :MLA_NOPE], kpe_s).reshape(bs * (ls + lc), -1),
                  kv_s[..., MLA_NOPE:].astype(BF16).reshape(bs * (ls + lc), -1), bsz=bs, lq=ls, lk=ls + lc)

    o_cat = jnp.concatenate([jnp.concatenate([o_na_p, o_na_s], 0), jnp.concatenate([o_mla_p, o_mla_s], 0)], -1)
    new = (kv_p32[:, :NA_QKV].reshape(bp, lp, NA_HEADS, NA_HEAD_DIM),
           kv_p32[:, NA_QKV:].reshape(bp, lp, NA_HEADS, NA_HEAD_DIM),
           ckv[:tp].reshape(bp, lp, -1), kpe[:tp].reshape(bp, lp, -1))
    return o_cat, new


def _modulation(cvec, w, b):
    return jnp.split(jax.nn.silu(cvec) @ w + b, 6, axis=-1)


def kernel(x_prompt, x_sample, c, cache_na_k, cache_na_v, cache_mla_ckv, cache_mla_kpe, state_ssm,
           c_ctx, w_ada, b_ada, ln_g, ln_b, attn_w_in, na_rpb, mla_q_norm, mla_w_uq, mla_kv_norm,
           mla_w_ukv, attn_w_out, ssm_w_in, ssm_conv_w, ssm_conv_b, ssm_dt_bias, ssm_a_log, ssm_d,
           ssm_norm, ssm_w_out, moe_w_router, moe_b_router, moe_w_gu, moe_b_gu, moe_w_down, moe_b_down):
    bp, lp, D = x_prompt.shape
    bs, ls, _ = x_sample.shape
    tp = bp * lp
    tiles_p, tiles_per_s = tp // ROW_TILE, ls // ROW_TILE
    new_k, new_v, new_ckv, new_kpe, new_ssm = [], [], [], [], []

    cond = jnp.concatenate([c_ctx[None, :], c], 0)
    mods = [_modulation(cond, w_ada[l], b_ada[l]) for l in range(DEPTH)]
    zero = jnp.zeros_like(mods[0][0])

    x = jnp.concatenate([x_prompt.reshape(tp, D), x_sample.reshape(bs * ls, D)], 0)
    shift1, scale1 = mods[0][0], mods[0][1]
    h = jnp.concatenate(
        [(x_prompt * (1.0 + scale1[0]) + shift1[0]).reshape(tp, D),
         (x_sample * (1.0 + scale1[1:, None, :]) + shift1[1:, None, :]).reshape(bs * ls, D)], 0).astype(BF16)

    for l in range(DEPTH):
        m = mods[l]
        if l % 2 == 0:
            i = l // 2
            act, (k_c, v_c, ckv_c, kpe_c) = _attn_mixer(
                h, bp, lp, bs, ls, cache_na_k[:, i], cache_na_v[:, i], cache_mla_ckv[:, i], cache_mla_kpe[:, i],
                attn_w_in[i], na_rpb[i], mla_q_norm[i], mla_w_uq[i], mla_kv_norm[i], mla_w_ukv[i])
            w_out = attn_w_out[i]
            new_k.append(k_c)
            new_v.append(v_c)
            new_ckv.append(ckv_c)
            new_kpe.append(kpe_c)
        else:
            j = l // 2
            act, st = _ssm_mixer(h, bp, lp, bs, ls, state_ssm[:, j], ssm_w_in[j], ssm_conv_w[j], ssm_conv_b[j],
                                 ssm_dt_bias[j], ssm_a_log[j], ssm_d[j], ssm_norm[j])
            w_out = ssm_w_out[j]
            new_ssm.append(st)
        x1, hm, top_e, gates = _proj_res_ln(act, w_out, x, _mod_table(m[2], m[3], m[4]),
                                            jnp.stack([ln_g[l, 0], ln_b[l, 0]]), moe_w_router[l], moe_b_router[l],
                                            tiles_p, tiles_per_s)
        yg = _moe(hm, top_e, _cast_bf16(moe_w_gu, l), moe_b_gu[l], _cast_bf16(moe_w_down, l), moe_b_down[l])
        nxt = mods[l + 1] if l + 1 < DEPTH else (zero, zero)
        x, h = _combine_ln(yg, gates, x1, _mod_table(m[5], nxt[0], nxt[1]),
                           jnp.stack([ln_g[l, 1], ln_b[l, 1]]), tiles_p, tiles_per_s)
    return (x[:tp].reshape(bp, lp, D), x[tp:].reshape(bs, ls, D), jnp.stack(new_k, 1), jnp.stack(new_v, 1),
            jnp.stack(new_ckv, 1), jnp.stack(new_kpe, 1), jnp.stack(new_ssm, 1))
```
